```python
import jax
import jax.numpy as jnp
from jax import lax
import numpy as np

D_MODEL = 1024
BATCH = 4
SEQ = 4096
DEPTH = 4

GRID_W = 64
CTX_LEN = 256
QBLOCK = 128
ROPE_THETA = 10000.0
EPS = 1e-6

MLA_HEADS = 8
MLA_Q_RANK = 256
MLA_KV_RANK = 128
MLA_NOPE = 64
MLA_ROPE = 32
MLA_V = 64
MLA_WIDTH = MLA_HEADS * MLA_V

GQA_HEADS = 8
GQA_KV_HEADS = 2
GQA_HEAD_DIM = 64
GQA_WIDTH = GQA_HEADS * GQA_HEAD_DIM

GDN_HEADS = 4
GDN_HEAD_DIM = 128
GDN_WIDTH = GDN_HEADS * GDN_HEAD_DIM
GDN_CONV = 4
GDN_CHUNK = 64

LRU_WIDTH = 512
LRU_BLOCKS = 8
LRU_BLOCK_W = LRU_WIDTH // LRU_BLOCKS
LRU_CONV = 4
LRU_C = 8.0

DEEPNORM_ALPHA = (2 * DEPTH) ** 0.25
DEEPNORM_BETA = (8 * DEPTH) ** -0.25

N_ATT = (DEPTH + 1) // 2
N_REC = DEPTH // 2

ATT_SPLITS = (MLA_Q_RANK, MLA_KV_RANK, MLA_ROPE, MLA_WIDTH,
              GQA_WIDTH, GQA_KV_HEADS * GQA_HEAD_DIM, GQA_KV_HEADS * GQA_HEAD_DIM, GQA_WIDTH)
ATT_IN = sum(ATT_SPLITS)
ATT_MIX = MLA_WIDTH + GQA_WIDTH
REC_SPLITS = (3 * GDN_WIDTH, GDN_WIDTH, 2 * GDN_HEADS, 2 * GDN_HEADS, LRU_WIDTH, LRU_WIDTH)
REC_IN = sum(REC_SPLITS)
REC_MIX = GDN_WIDTH + LRU_WIDTH

kernel_name = 'hybrid_mla_gqa_gdn_rglru_dit_trunk'


def _split(p, sizes):
    idx = np.cumsum(sizes)[:-1].tolist()
    return jnp.split(p, idx, axis=-1)


def layer_norm(x, g, b):
    xf = x.astype(jnp.float32)
    mu = jnp.mean(xf, -1, keepdims=True)
    var = jnp.mean(jnp.square(xf - mu), -1, keepdims=True)
    return ((xf - mu) * lax.rsqrt(var + EPS) * g + b).astype(x.dtype)


def rms_norm(x, g):
    xf = x.astype(jnp.float32)
    return (xf * lax.rsqrt(jnp.mean(jnp.square(xf), -1, keepdims=True) + EPS) * g).astype(x.dtype)


def l2_norm(x):
    return x * lax.rsqrt(jnp.sum(jnp.square(x), -1, keepdims=True) + EPS)


def axial_rope_tables(row, col, dim):
    half = dim // 2
    inv = ROPE_THETA ** (-jnp.arange(0, half, 2, dtype=jnp.float32) / half)
    ang_r = row.astype(jnp.float32)[:, None] * inv
    ang_c = col.astype(jnp.float32)[:, None] * inv
    return (jnp.cos(ang_r), jnp.sin(ang_r), jnp.cos(ang_c), jnp.sin(ang_c))


def _rotate(x, cos, sin):
    x1, x2 = jnp.split(x, 2, axis=-1)
    cos = cos[None, :, None, :].astype(x.dtype)
    sin = sin[None, :, None, :].astype(x.dtype)
    return jnp.concatenate([x1 * cos - x2 * sin, x1 * sin + x2 * cos], axis=-1)


def apply_axial_rope(x, tabs):
    cr, sr, cc, sc = tabs
    xr, xc = jnp.split(x, 2, axis=-1)
    return jnp.concatenate([_rotate(xr, cr, sr), _rotate(xc, cc, sc)], axis=-1)


def block_attention(q, k, v):
    bsz, s, hk, grp, dk = q.shape
    scale = dk ** -0.5
    nb = s // QBLOCK
    qb = jnp.moveaxis(q.reshape(bsz, nb, QBLOCK, hk, grp, dk), 1, 0)

    def one_block(qblk):
        sc = jnp.einsum('bqhgd,bkhd->bhgqk', qblk, k, preferred_element_type=jnp.float32) * scale
        p = jax.nn.softmax(sc, axis=-1).astype(v.dtype)
        return jnp.einsum('bhgqk,bkhd->bqhgd', p, v)

    o = lax.map(one_block, qb)
    return jnp.moveaxis(o, 0, 1).reshape(bsz, s, hk, grp, v.shape[-1])


def _attn_project(h, w_in, q_norm_a, w_uq, kv_norm_a, w_ukv, q_norm_b, k_norm_b, rope_a, rope_b):
    bsz, t, _ = h.shape
    cq, ckv, kr, ga, qb, kb, vb, gb = _split(h @ w_in, ATT_SPLITS)
    qa = (rms_norm(cq, q_norm_a) @ w_uq).reshape(bsz, t, MLA_HEADS, MLA_NOPE + MLA_ROPE)
    kv = (rms_norm(ckv, kv_norm_a) @ w_ukv).reshape(bsz, t, MLA_HEADS, MLA_NOPE + MLA_V)
    q_nope, q_rope = qa[..., :MLA_NOPE], qa[..., MLA_NOPE:]
    k_nope, va = kv[..., :MLA_NOPE], kv[..., MLA_NOPE:]
    kr = kr[:, :, None, :]
    qb = rms_norm(qb.reshape(bsz, t, GQA_HEADS, GQA_HEAD_DIM), q_norm_b)
    kb = rms_norm(kb.reshape(bsz, t, GQA_KV_HEADS, GQA_HEAD_DIM), k_norm_b)
    vb = vb.reshape(bsz, t, GQA_KV_HEADS, GQA_HEAD_DIM)
    if rope_a is not None:
        q_rope = apply_axial_rope(q_rope, rope_a)
        kr = apply_axial_rope(kr, rope_a)
        qb = apply_axial_rope(qb, rope_b)
        kb = apply_axial_rope(kb, rope_b)
    qa = jnp.concatenate([q_nope, q_rope], -1)[:, :, :, None, :]
    ka = jnp.concatenate([k_nope, jnp.broadcast_to(kr, (bsz, t, MLA_HEADS, MLA_ROPE))], -1)
    qb = qb.reshape(bsz, t, GQA_KV_HEADS, GQA_HEADS // GQA_KV_HEADS, GQA_HEAD_DIM)
    return qa, ka, va, ga, qb, kb, vb, gb


def _attn_out(oa, ga, ob, gb, w_out):
    bsz, t = ga.shape[:2]
    mix = jnp.concatenate([oa.reshape(bsz, t, MLA_WIDTH) * jax.nn.silu(ga),
                           ob.reshape(bsz, t, GQA_WIDTH) * jax.nn.silu(gb)], -1)
    return mix @ w_out


def attention_mixer(h, h_ctx, w_in, w_out, q_norm_a, w_uq, kv_norm_a, w_ukv, q_norm_b, k_norm_b,
                    rope_a, rope_b, with_ctx):
    qa, ka, va, ga, qb, kb, vb, gb = _attn_project(h, w_in, q_norm_a, w_uq, kv_norm_a, w_ukv,
                                                   q_norm_b, k_norm_b, rope_a, rope_b)
    qa_c, ka_c, va_c, ga_c, qb_c, kb_c, vb_c, gb_c = _attn_project(h_ctx, w_in, q_norm_a, w_uq, kv_norm_a,
                                                                   w_ukv, q_norm_b, k_norm_b, None, None)
    oa = block_attention(qa, jnp.concatenate([ka_c, ka], 1), jnp.concatenate([va_c, va], 1))
    ob = block_attention(qb, jnp.concatenate([kb_c, kb], 1), jnp.concatenate([vb_c, vb], 1))
    y = _attn_out(oa, ga, ob, gb, w_out)
    y_ctx = None
    if with_ctx:
        y_ctx = _attn_out(block_attention(qa_c, ka_c, va_c), ga_c,
                          block_attention(qb_c, kb_c, vb_c), gb_c, w_out)
    return y, y_ctx


def dwconv_centered(x, w, b=None):
    ksz, ch = w.shape
    y = lax.conv_general_dilated(x, w[:, None, :].astype(x.dtype), window_strides=(1,),
                                 padding=[((ksz - 1) // 2, ksz // 2)],
                                 dimension_numbers=('NWC', 'WIO', 'NWC'), feature_group_count=ch)
    return y if b is None else y + b


def chunk_gated_delta(q, k, v, g, beta, s0):
    bsz, t, nh, dk = q.shape
    dv = v.shape[-1]
    csz = GDN_CHUNK
    nc = t // csz

    def chunks(a):
        a = a.reshape((bsz, nc, csz, nh) + a.shape[3:])
        return jnp.moveaxis(jnp.moveaxis(a, 1, 0), 2, 3)

    q = chunks(q) * (dk ** -0.5)
    k = chunks(k)
    v = chunks(v)
    beta = chunks(beta)
    g_cum = jnp.cumsum(chunks(g), axis=-1)
    idx = jnp.arange(csz)
    causal = idx[:, None] >= idx[None, :]
    strict = idx[:, None] > idx[None, :]
    decay = jnp.exp(jnp.where(causal, g_cum[..., :, None] - g_cum[..., None, :], -jnp.inf))
    kb = k * beta[..., None]
    m = jnp.where(strict, jnp.einsum('nbhid,nbhjd->nbhij', kb, k) * decay, 0.0)
    a_mat = m + jnp.eye(csz, dtype=m.dtype)
    rhs = jnp.concatenate([v * beta[..., None], kb * jnp.exp(g_cum)[..., None]], -1)
    sol = lax.linalg.triangular_solve(a_mat, rhs, left_side=True, lower=True, unit_diagonal=True)
    u, w = sol[..., :dv], sol[..., dv:]
    attn = jnp.einsum('nbhid,nbhjd->nbhij', q, k) * decay
    q_dec = q * jnp.exp(g_cum)[..., None]
    k_dec = k * jnp.exp(g_cum[..., -1:] - g_cum)[..., None]
    g_last = jnp.exp(g_cum[..., -1])

    def step(s, xs):
        qd, kd, u_n, w_n, at, gl = xs
        v_new = u_n - jnp.einsum('bhck,bhkv->bhcv', w_n, s)
        o = jnp.einsum('bhck,bhkv->bhcv', qd, s) + jnp.einsum('bhij,bhjv->bhiv', at, v_new)
        s = s * gl[..., None, None] + jnp.einsum('bhck,bhcv->bhkv', kd, v_new)
        return s, o

    s_fin, o = lax.scan(step, s0, (q_dec, k_dec, u, w, attn, g_last))
    o = jnp.transpose(o, (1, 0, 3, 2, 4)).reshape(bsz, t, nh, dv)
    return o, s_fin


def gdn_scan(q, k, v, g, beta, s0, reverse):
    if reverse:
        q, k, v, g, beta = (jnp.flip(a, 1) for a in (q, k, v, g, beta))
    o, s_fin = chunk_gated_delta(q, k, v, g, beta, s0)
    if reverse:
        o = jnp.flip(o, 1)
    return o, s_fin


def linear_scan(a, b, h0):
    def combine(l, r):
        return l[0] * r[0], r[0] * l[1] + r[1]
    a_cum, b_cum = lax.associative_scan(combine, (a, b), axis=1)
    return a_cum * h0[:, None, :] + b_cum


def lru_scan(a, b, h0, reverse):
    if reverse:
        a, b = jnp.flip(a, 1), jnp.flip(b, 1)
    h = linear_scan(a, b, h0)
    h_fin = h[:, -1]
    if reverse:
        h = jnp.flip(h, 1)
    return h, h_fin


def lru_coeffs(xl, w_gate, b_gate, lam):
    bsz, t, _ = xl.shape
    xb = xl.reshape(bsz, t, LRU_BLOCKS, LRU_BLOCK_W)
    gates = jnp.einsum('btnc,gncd->btgnd', xb, w_gate.astype(jnp.float32)).reshape(bsz, t, 2, LRU_WIDTH)
    gates = gates + b_gate.astype(jnp.float32)
    r = jax.nn.sigmoid(gates[:, :, 0])
    i = jax.nn.sigmoid(gates[:, :, 1])
    log_a = LRU_C * r * jax.nn.log_sigmoid(lam.astype(jnp.float32))
    a = jnp.exp(log_a)
    b = jnp.sqrt(-jnp.expm1(2.0 * log_a)) * (i * xl)
    return a, b


def _rec_inputs(h, w_in, gdn_conv_w, gdn_a_log, gdn_dt_bias, lru_conv_w, lru_conv_b):
    bsz, t, _ = h.shape
    qkv, z, b, a, xr, gr = _split(h @ w_in, REC_SPLITS)
    qkv = jax.nn.silu(dwconv_centered(qkv, gdn_conv_w)).astype(jnp.float32)
    q, k, v = (u.reshape(bsz, t, GDN_HEADS, GDN_HEAD_DIM) for u in jnp.split(qkv, 3, axis=-1))
    q, k = l2_norm(q), l2_norm(k)
    beta = jax.nn.sigmoid(b.astype(jnp.float32)).reshape(bsz, t, 2, GDN_HEADS)
    g = -jnp.exp(gdn_a_log.astype(jnp.float32)) * jax.nn.softplus(
        a.astype(jnp.float32).reshape(bsz, t, 2, GDN_HEADS) + gdn_dt_bias.astype(jnp.float32))
    xl = dwconv_centered(xr, lru_conv_w, lru_conv_b).astype(jnp.float32)
    return q, k, v, beta, g, z, xl, gr


def _rec_out(o, z, r, gr, gdn_norm, w_out, dtype):
    bsz, t = z.shape[:2]
    og = rms_norm(o, gdn_norm) * jax.nn.silu(z.astype(jnp.float32)).reshape(bsz, t, GDN_HEADS, GDN_HEAD_DIM)
    y_lru = r * jax.nn.silu(gr.astype(jnp.float32))
    mix = jnp.concatenate([og.reshape(bsz, t, GDN_WIDTH), y_lru], -1).astype(dtype)
    return mix @ w_out


def recurrent_mixer(h, h_ctx, w_in, w_out, gdn_conv_w, gdn_a_log, gdn_dt_bias, gdn_norm,
                    lru_conv_w, lru_conv_b, lru_gate_w, lru_gate_b, lru_lambda, with_ctx):
    q_l, k_l, v_l, beta_l, g_l, z_l, xl_l, gr_l = _rec_inputs(h, w_in, gdn_conv_w, gdn_a_log, gdn_dt_bias,
                                                              lru_conv_w, lru_conv_b)
    q_c, k_c, v_c, beta_c, g_c, z_c, xl_c, gr_c = _rec_inputs(h_ctx, w_in, gdn_conv_w, gdn_a_log, gdn_dt_bias,
                                                              lru_conv_w, lru_conv_b)
    bsz = h.shape[0]
    s0 = jnp.zeros((bsz, GDN_HEADS, GDN_HEAD_DIM, GDN_HEAD_DIM), jnp.float32)
    h0 = jnp.zeros((bsz, LRU_WIDTH), jnp.float32)
    o_l = o_c = r_l = r_c = 0.0
    for d in range(2):
        rev = d == 1
        oc, s_ctx = gdn_scan(q_c, k_c, v_c, g_c[:, :, d], beta_c[:, :, d], s0, rev)
        ol, _ = gdn_scan(q_l, k_l, v_l, g_l[:, :, d], beta_l[:, :, d], s_ctx, rev)
        a_c, b_c = lru_coeffs(xl_c, lru_gate_w[d], lru_gate_b[d], lru_lambda[d])
        hc, h_ctx_fin = lru_scan(a_c, b_c, h0, rev)
        a_l, b_l = lru_coeffs(xl_l, lru_gate_w[d], lru_gate_b[d], lru_lambda[d])
        hl, _ = lru_scan(a_l, b_l, h_ctx_fin, rev)
        o_l, o_c, r_l, r_c = o_l + ol, o_c + oc, r_l + hl, r_c + hc
    y = _rec_out(o_l, z_l, r_l, gr_l, gdn_norm, w_out, h.dtype)
    y_ctx = _rec_out(o_c, z_c, r_c, gr_c, gdn_norm, w_out, h.dtype) if with_ctx else None
    return y, y_ctx


def setup_inputs(seed: int = 0) -> dict:
    key = jax.random.key(seed)
    keys = iter(jax.random.split(key, 40))
    f32 = jnp.float32

    def nrm(shape, scale):
        return jax.random.normal(next(keys), shape, f32) * scale

    def unif(shape, lo, hi):
        return jax.random.uniform(next(keys), shape, f32, lo, hi)

    d = D_MODEL
    x = nrm((BATCH, SEQ, d), 1.0)
    c = nrm((BATCH, d), 1.0)
    ctx = nrm((BATCH, CTX_LEN, d), 1.0)
    c_ctx = nrm((d,), 1.0)
    mod_w = nrm((DEPTH, d, 3 * d), d ** -0.5)
    mod_b = nrm((DEPTH, 3 * d), 0.02)
    ln_g = 1.0 + nrm((DEPTH, d), 0.02)
    ln_b = nrm((DEPTH, d), 0.02)
    att_w_in = nrm((N_ATT, d, ATT_IN), d ** -0.5)
    att_w_out = nrm((N_ATT, ATT_MIX, d), ATT_MIX ** -0.5 * DEEPNORM_BETA)
    mla_q_norm = 1.0 + nrm((N_ATT, MLA_Q_RANK), 0.02)
    mla_w_uq = nrm((N_ATT, MLA_Q_RANK, MLA_HEADS * (MLA_NOPE + MLA_ROPE)), MLA_Q_RANK ** -0.5)
    mla_kv_norm = 1.0 + nrm((N_ATT, MLA_KV_RANK), 0.02)
    mla_w_ukv = nrm((N_ATT, MLA_KV_RANK, MLA_HEADS * (MLA_NOPE + MLA_V)), MLA_KV_RANK ** -0.5)
    gqa_q_norm = 1.0 + nrm((N_ATT, GQA_HEAD_DIM), 0.02)
    gqa_k_norm = 1.0 + nrm((N_ATT, GQA_HEAD_DIM), 0.02)
    rec_w_in = nrm((N_REC, d, REC_IN), d ** -0.5)
    rec_w_out = nrm((N_REC, REC_MIX, d), REC_MIX ** -0.5 * DEEPNORM_BETA)
    gdn_conv_w = nrm((N_REC, GDN_CONV, 3 * GDN_WIDTH), GDN_CONV ** -0.5)
    gdn_a_log = jnp.log(unif((N_REC, 2, GDN_HEADS), 1.0, 16.0))
    dt = jnp.exp(unif((N_REC, 2, GDN_HEADS), float(np.log(1e-3)), float(np.log(1e-1))))
    gdn_dt_bias = dt + jnp.log(-jnp.expm1(-dt))
    gdn_norm = 1.0 + nrm((N_REC, GDN_HEAD_DIM), 0.02)
    lru_conv_w = nrm((N_REC, LRU_CONV, LRU_WIDTH), LRU_CONV ** -0.5)
    lru_conv_b = nrm((N_REC, LRU_WIDTH), 0.02)
    lru_gate_w = nrm((N_REC, 2, 2, LRU_BLOCKS, LRU_BLOCK_W, LRU_BLOCK_W), LRU_BLOCK_W ** -0.5)
    lru_gate_b = nrm((N_REC, 2, 2, LRU_WIDTH), 0.02)
    a_c = unif((N_REC, 2, LRU_WIDTH), 0.9, 0.999) ** (1.0 / LRU_C)
    lru_lambda = jnp.log(a_c) - jnp.log1p(-a_c)
    return {'x': x, 'c': c, 'ctx': ctx, 'c_ctx': c_ctx, 'mod_w': mod_w, 'mod_b': mod_b,
            'ln_g': ln_g, 'ln_b': ln_b, 'att_w_in': att_w_in, 'att_w_out': att_w_out,
            'mla_q_norm': mla_q_norm, 'mla_w_uq': mla_w_uq, 'mla_kv_norm': mla_kv_norm,
            'mla_w_ukv': mla_w_ukv, 'gqa_q_norm': gqa_q_norm, 'gqa_k_norm': gqa_k_norm,
            'rec_w_in': rec_w_in, 'rec_w_out': rec_w_out, 'gdn_conv_w': gdn_conv_w,
            'gdn_a_log': gdn_a_log, 'gdn_dt_bias': gdn_dt_bias, 'gdn_norm': gdn_norm,
            'lru_conv_w': lru_conv_w, 'lru_conv_b': lru_conv_b, 'lru_gate_w': lru_gate_w,
            'lru_gate_b': lru_gate_b, 'lru_lambda': lru_lambda}


def reference(x, c, ctx, c_ctx, mod_w, mod_b, ln_g, ln_b, att_w_in, att_w_out, mla_q_norm, mla_w_uq,
              mla_kv_norm, mla_w_ukv, gqa_q_norm, gqa_k_norm, rec_w_in, rec_w_out, gdn_conv_w,
              gdn_a_log, gdn_dt_bias, gdn_norm, lru_conv_w, lru_conv_b, lru_gate_w, lru_gate_b, lru_lambda):
    n = x.shape[1]
    rows = n // GRID_W
    row = jnp.repeat(jnp.arange(rows), GRID_W)
    col = jnp.tile(jnp.arange(GRID_W), rows)
    rope_mla = axial_rope_tables(row, col, MLA_ROPE)
    rope_gqa = axial_rope_tables(row, col, GQA_HEAD_DIM)
    sc = jax.nn.silu(c)
    sc_ctx = jax.nn.silu(c_ctx)
    for layer in range(DEPTH):
        last = layer == DEPTH - 1
        shift, scale, gate = jnp.split(sc @ mod_w[layer] + mod_b[layer], 3, axis=-1)
        shift_c, scale_c, gate_c = jnp.split(sc_ctx @ mod_w[layer] + mod_b[layer], 3, axis=-1)
        h = x * (1.0 + scale[:, None]) + shift[:, None]
        h_ctx = ctx * (1.0 + scale_c) + shift_c
        li = layer // 2
        if layer % 2 == 0:
            y, y_ctx = attention_mixer(h, h_ctx, att_w_in[li], att_w_out[li], mla_q_norm[li], mla_w_uq[li],
                                       mla_kv_norm[li], mla_w_ukv[li], gqa_q_norm[li], gqa_k_norm[li],
                                       rope_mla, rope_gqa, not last)
        else:
            y, y_ctx = recurrent_mixer(h, h_ctx, rec_w_in[li], rec_w_out[li], gdn_conv_w[li], gdn_a_log[li],
                                       gdn_dt_bias[li], gdn_norm[li], lru_conv_w[li], lru_conv_b[li],
                                       lru_gate_w[li], lru_gate_b[li], lru_lambda[li], not last)
        x = layer_norm(DEEPNORM_ALPHA * x + gate[:, None] * y, ln_g[layer], ln_b[layer])
        if not last:
            ctx = layer_norm(DEEPNORM_ALPHA * ctx + gate_c * y_ctx, ln_g[layer], ln_b[layer])
    return x
```

```python
import functools

import numpy as np
import jax
import jax.numpy as jnp
from jax import lax
from jax.experimental import pallas as pl
from jax.experimental.pallas import tpu as pltpu

F32 = jnp.float32
BF16 = jnp.bfloat16

D_MODEL = 1024
DEPTH = 4
GRID_W = 64
CTX_LEN = 256
ROPE_THETA = 10000.0
EPS = 1e-6

MLA_HEADS = 8
MLA_Q_RANK = 256
MLA_KV_RANK = 128
MLA_NOPE = 64
MLA_ROPE = 32
MLA_V = 64
MLA_WIDTH = MLA_HEADS * MLA_V
GQA_HEADS = 8
GQA_KV_HEADS = 2
GQA_HEAD_DIM = 64
GQA_WIDTH = GQA_HEADS * GQA_HEAD_DIM
GDN_HEADS = 4
GDN_HEAD_DIM = 128
GDN_WIDTH = GDN_HEADS * GDN_HEAD_DIM
LRU_WIDTH = 512
LRU_BLOCKS = 8
LRU_BLOCK_W = LRU_WIDTH // LRU_BLOCKS
LRU_C = 8.0
DEEPNORM_ALPHA = (2 * DEPTH) ** 0.25

LANES = 128
SUBLANES = 8
TM = 256
GDN_CHUNK = 128
GDN_DIAG = 16
VMEM_LIMIT = 48 * 1024 * 1024

ATT_COLS = 2560
REC_CONV_COLS = 3 * GDN_WIDTH + LRU_WIDTH
REC_COLS = REC_CONV_COLS + 2 * 512 + LANES


def _params(*sem):
    return pltpu.CompilerParams(dimension_semantics=sem, vmem_limit_bytes=VMEM_LIMIT)


def _bdot(a, b):
    return jnp.dot(a.astype(BF16), b.astype(BF16), preferred_element_type=F32)


def _bdot_nt(a, b):
    return lax.dot_general(a.astype(BF16), b.astype(BF16), (((1,), (1,)), ((), ())),
                           preferred_element_type=F32)


def _split_bf16(a, parts):
    out = []
    for _ in range(parts):
        hi = a.astype(BF16)
        out.append(hi)
        a = a - hi.astype(F32)
    return out


def _dot_split(a, b):
    a_hi, a_lo = _split_bf16(a, 2)
    b_hi, b_lo = _split_bf16(b, 2)
    d = functools.partial(jnp.dot, preferred_element_type=F32)
    return d(a_hi, b_hi) + (d(a_lo, b_hi) + d(a_hi, b_lo))


def _silu(x):
    return x * jax.nn.sigmoid(x)


def _lane_iota(shape):
    return lax.broadcasted_iota(jnp.int32, shape, len(shape) - 1)


def _swap_groups(x, n):
    fwd = pltpu.roll(x, LANES - n, 1)
    bwd = pltpu.roll(x, n, 1)
    return jnp.where((_lane_iota(x.shape) % (2 * n)) < n, fwd, bwd)


def _rope(x, cos, sin, n):
    return x * cos + _swap_groups(x, n) * sin


def _layer_norm(z, g, b):
    mu = jnp.mean(z, -1, keepdims=True)
    zc = z - mu
    var = jnp.mean(zc * zc, -1, keepdims=True)
    return zc * lax.rsqrt(var + EPS) * g + b


def _mod_kernel(c_ref, w_ref, b_ref, o_ref):
    c = c_ref[...]
    o_ref[0] = _dot_split(_silu(c), w_ref[0]) + b_ref[0]


def _modulation(c, c_ctx, mod_w, mod_b):
    bsz = c.shape[0]
    rows = jnp.concatenate([c, c_ctx[None, :], jnp.zeros((SUBLANES - bsz - 1, D_MODEL), F32)], 0)
    tn = 1024
    return pl.pallas_call(
        _mod_kernel,
        grid=(DEPTH, 3 * D_MODEL // tn),
        in_specs=[pl.BlockSpec((SUBLANES, D_MODEL), lambda l, n: (0, 0)),
                  pl.BlockSpec((1, D_MODEL, tn), lambda l, n: (l, 0, n)),
                  pl.BlockSpec((1, 1, tn), lambda l, n: (l, 0, n))],
        out_specs=pl.BlockSpec((1, SUBLANES, tn), lambda l, n: (l, 0, n)),
        out_shape=jax.ShapeDtypeStruct((DEPTH, SUBLANES, 3 * D_MODEL), F32),
        compiler_params=_params("parallel", "parallel"),
        name="modulation",
    )(rows, mod_w, mod_b.reshape(DEPTH, 1, 3 * D_MODEL))


def _mod_select(mods_l, bsz):
    m3 = mods_l.reshape(SUBLANES, 3, D_MODEL)
    ctx_m = jnp.broadcast_to(m3[bsz], (bsz, 3, D_MODEL))
    sel = jnp.stack([ctx_m, m3[:bsz]], 1)
    return jnp.pad(sel, ((0, 0), (0, 0), (0, SUBLANES - 3), (0, 0)))


def _mod_spec(off=0):
    return pl.BlockSpec((1, 1, SUBLANES, D_MODEL), lambda b, i: (b, jnp.minimum(i + off, 1), 0, 0))


def _att_proj_kernel(x_ref, m_ref, win_ref, wuq_ref, wk_ref, wv_ref, nq_ref, nkv_ref, nqb_ref, nkb_ref,
                     ra_ref, rb_ref, qa_ref, ka_ref, va_ref, qb_ref, kb_ref, vb_ref, sg_ref):
    shift = m_ref[0, 0, 0:1, :]
    scale = m_ref[0, 0, 1:2, :]
    h = x_ref[0] * (1.0 + scale) + shift
    p = _bdot(h, win_ref[...])
    cq, ckv, kr = p[:, 0:256], p[:, 256:384], p[:, 384:512]
    gates = p[:, 512:1536]
    qb, kb, vb = p[:, 1536:2048], p[:, 2048:2304], p[:, 2304:2560]
    sg_ref[0] = _silu(gates).astype(BF16)

    def rms(x):
        return x * lax.rsqrt(jnp.mean(x * x, -1, keepdims=True) + EPS)

    qa = _bdot(rms(cq) * nq_ref[...], wuq_ref[...])
    ckvn = rms(ckv) * nkv_ref[...]
    kn = _bdot(ckvn, wk_ref[...])
    va_ref[0] = _bdot(ckvn, wv_ref[...]).astype(BF16)
    cos_a, sin_a = ra_ref[0], ra_ref[1]
    kr = _rope(kr, cos_a, sin_a, MLA_ROPE // 4)
    q_scale = (MLA_NOPE + MLA_ROPE) ** -0.5
    for hd in range(MLA_HEADS):
        sl = slice(hd * LANES, (hd + 1) * LANES)
        qa_ref[0, hd] = (_rope(qa[:, sl], cos_a, sin_a, MLA_ROPE // 4) * q_scale).astype(BF16)
        ka_ref[0, hd] = (kn[:, sl] + kr).astype(BF16)

    cos_b, sin_b = rb_ref[0], rb_ref[1]
    low = _lane_iota((TM, LANES)) < GQA_HEAD_DIM
    for c in range(GQA_HEADS // 2):
        sl = slice(c * LANES, (c + 1) * LANES)
        xq = qb[:, sl]
        sq = xq * xq
        s_lo = jnp.sum(jnp.where(low, sq, 0.0), -1, keepdims=True)
        s_hi = jnp.sum(jnp.where(low, 0.0, sq), -1, keepdims=True)
        ms = jnp.where(low, s_lo, s_hi) * (1.0 / GQA_HEAD_DIM)
        xn = xq * lax.rsqrt(ms + EPS) * nqb_ref[:, sl]
        qb_ref[0, :, sl] = (_rope(xn, cos_b, sin_b, GQA_HEAD_DIM // 4) * GQA_HEAD_DIM ** -0.5).astype(BF16)
    for g in range(GQA_KV_HEADS):
        sl = slice(g * LANES, (g + 1) * LANES)
        kb_ref[0, g] = _rope(rms(kb[:, sl]) * nkb_ref[...], cos_b, sin_b, GQA_HEAD_DIM // 4).astype(BF16)
        vb_ref[0, g] = vb[:, sl].astype(BF16)


def _softmax_pv(q, k, v):
    s = _bdot_nt(q, k)
    m = jnp.max(s, -1, keepdims=True)
    p = jnp.exp(s - m)
    l = jnp.sum(p, -1, keepdims=True)
    return jnp.dot(p.astype(BF16), v, preferred_element_type=F32) / l


def _mla_attn_kernel(q_ref, k_ref, v_ref, sg_ref, o_ref, *, n_all):
    low = _lane_iota((TM, LANES)) < MLA_V

    def run(nk):
        oa = _softmax_pv(q_ref[0, 0], k_ref[0, 0, 0:nk, :], v_ref[0, 0:nk, :])
        ob = _softmax_pv(q_ref[0, 1], k_ref[0, 1, 0:nk, :], v_ref[0, 0:nk, :])
        o_ref[0] = (jnp.where(low, oa, ob) * sg_ref[0].astype(F32)).astype(BF16)

    @pl.when(pl.program_id(2) == 0)
    def _():
        run(CTX_LEN)

    @pl.when(pl.program_id(2) > 0)
    def _():
        run(n_all)


def _gqa_attn_kernel(q_ref, k_ref, v_ref, sg_ref, o_ref, *, n_all):
    low = _lane_iota((TM, LANES)) < GQA_HEAD_DIM

    def run(nk):
        q = q_ref[0]
        zero = jnp.zeros_like(q)
        k = k_ref[0, 0, 0:nk, :]
        v = v_ref[0, 0, 0:nk, :]
        oa = _softmax_pv(jnp.where(low, q, zero), k, v)
        ob = _softmax_pv(jnp.where(low, zero, q), k, v)
        o_ref[0] = (jnp.where(low, oa, ob) * sg_ref[0].astype(F32)).astype(BF16)

    @pl.when(pl.program_id(2) == 0)
    def _():
        run(CTX_LEN)

    @pl.when(pl.program_id(2) > 0)
    def _():
        run(n_all)


def _att_out_kernel(ma_ref, mb_ref, w_ref, x_ref, m_ref, g_ref, b_ref, o_ref):
    y = (jnp.dot(ma_ref[0], w_ref[0:MLA_WIDTH, :], preferred_element_type=F32)
         + jnp.dot(mb_ref[0], w_ref[MLA_WIDTH:, :], preferred_element_type=F32))
    gate = m_ref[0, 0, 2:3, :]
    o_ref[0] = _layer_norm(DEEPNORM_ALPHA * x_ref[0] + gate * y, g_ref[...], b_ref[...])


def _rope_tables(n_seq):
    pos = jnp.arange(n_seq)
    row = (pos // GRID_W).astype(F32)[:, None]
    col = (pos % GRID_W).astype(F32)[:, None]

    def quarter(dim):
        half = dim // 2
        inv = ROPE_THETA ** (-jnp.arange(0, half, 2, dtype=F32) / half)
        return jnp.cos(row * inv), jnp.sin(row * inv), jnp.cos(col * inv), jnp.sin(col * inv)

    def with_ctx(cos, sin):
        cos = jnp.concatenate([jnp.ones((CTX_LEN, LANES), F32), cos], 0)
        sin = jnp.concatenate([jnp.zeros((CTX_LEN, LANES), F32), sin], 0)
        return jnp.stack([cos, sin], 0)

    cr, sr, cc, sc = quarter(MLA_ROPE)
    one = jnp.ones((n_seq, 1), F32)
    cos_a = jnp.concatenate([one * jnp.ones((1, MLA_NOPE)), cr, cr, cc, cc, one * jnp.ones((1, 32))], 1)
    sin_a = jnp.concatenate([one * jnp.zeros((1, MLA_NOPE)), -sr, sr, -sc, sc, one * jnp.zeros((1, 32))], 1)
    cr, sr, cc, sc = quarter(GQA_HEAD_DIM)
    cos_b = jnp.concatenate([cr, cr, cc, cc] * 2, 1)
    sin_b = jnp.concatenate([-sr, sr, -sc, sc] * 2, 1)
    return with_ctx(cos_a, sin_a), with_ctx(cos_b, sin_b)


def _att_weights(w_in, w_uq, w_ukv):
    d = D_MODEL
    o = np.cumsum((0, MLA_Q_RANK, MLA_KV_RANK, MLA_ROPE, MLA_WIDTH, GQA_WIDTH, 128, 128, GQA_WIDTH))
    cq, ckv, kr, ga, qb, kb, vb, gb = (w_in[:, o[i]:o[i + 1]] for i in range(8))
    z = lambda n: jnp.zeros((d, n), F32)
    dup = lambda w: jnp.concatenate([w[:, 0:64], w[:, 0:64], w[:, 64:128], w[:, 64:128]], 1)
    win = jnp.concatenate([cq, ckv, z(MLA_NOPE), kr, z(32), ga, gb, qb, dup(kb), dup(vb)], 1)
    wq = w_uq.reshape(MLA_Q_RANK, MLA_HEADS, MLA_NOPE + MLA_ROPE)
    wq = jnp.pad(wq, ((0, 0), (0, 0), (0, 32))).reshape(MLA_Q_RANK, MLA_HEADS * LANES)
    wkv = w_ukv.reshape(MLA_KV_RANK, MLA_HEADS, MLA_NOPE + MLA_V)
    wk = jnp.pad(wkv[:, :, :MLA_NOPE], ((0, 0), (0, 0), (0, 64))).reshape(MLA_KV_RANK, MLA_HEADS * LANES)
    wv = wkv[:, :, MLA_NOPE:].reshape(MLA_KV_RANK, MLA_WIDTH)
    return win.astype(BF16), wq.astype(BF16), wk.astype(BF16), wv.astype(BF16)


def _attention_layer(xs, msel, w_in, w_out, q_norm_a, w_uq, kv_norm_a, w_ukv, q_norm_b, k_norm_b,
                     rope_a, rope_b, ln_g, ln_b, last):
    bsz, n_all, d = xs.shape
    nt = n_all // TM
    win, wq, wk, wv = _att_weights(w_in, w_uq, w_ukv)
    full = lambda shape: pl.BlockSpec(shape, lambda b, i: (0,) * len(shape))
    tok = lambda w: pl.BlockSpec((1, TM, w), lambda b, i: (b, i, 0))
    head = lambda n: pl.BlockSpec((1, n, TM, LANES), lambda b, i: (b, 0, i, 0))
    rope = pl.BlockSpec((2, TM, LANES), lambda b, i: (0, i, 0))
    sds = jax.ShapeDtypeStruct
    qa, ka, va, qb, kb, vb, sg = pl.pallas_call(
        _att_proj_kernel,
        grid=(bsz, nt),
        in_specs=[tok(d), _mod_spec(), full((d, ATT_COLS)), full((MLA_Q_RANK, MLA_HEADS * LANES)),
                  full((MLA_KV_RANK, MLA_HEADS * LANES)), full((MLA_KV_RANK, MLA_WIDTH)),
                  full((1, MLA_Q_RANK)), full((1, MLA_KV_RANK)), full((1, GQA_WIDTH)), full((1, LANES)),
                  rope, rope],
        out_specs=[head(MLA_HEADS), head(MLA_HEADS), tok(MLA_WIDTH), tok(GQA_WIDTH),
                   head(GQA_KV_HEADS), head(GQA_KV_HEADS), tok(2 * 512)],
        out_shape=[sds((bsz, MLA_HEADS, n_all, LANES), BF16), sds((bsz, MLA_HEADS, n_all, LANES), BF16),
                   sds((bsz, n_all, MLA_WIDTH), BF16), sds((bsz, n_all, GQA_WIDTH), BF16),
                   sds((bsz, GQA_KV_HEADS, n_all, LANES), BF16), sds((bsz, GQA_KV_HEADS, n_all, LANES), BF16),
                   sds((bsz, n_all, 2 * 512), BF16)],
        compiler_params=_params("parallel", "parallel"),
        name="att_proj",
    )(xs, msel, win, wq, wk, wv, q_norm_a[None, :], kv_norm_a[None, :],
      jnp.tile(q_norm_b, GQA_HEADS)[None, :], jnp.tile(k_norm_b, 2)[None, :], rope_a, rope_b)

    pairs = MLA_HEADS // 2
    mix_a = pl.pallas_call(
        functools.partial(_mla_attn_kernel, n_all=n_all),
        grid=(bsz, pairs, nt),
        in_specs=[pl.BlockSpec((1, 2, TM, LANES), lambda b, j, i: (b, j, i, 0)),
                  pl.BlockSpec((1, 2, n_all, LANES), lambda b, j, i: (b, j, 0, 0)),
                  pl.BlockSpec((1, n_all, LANES), lambda b, j, i: (b, 0, j)),
                  pl.BlockSpec((1, TM, LANES), lambda b, j, i: (b, i, j))],
        out_specs=pl.BlockSpec((1, TM, LANES), lambda b, j, i: (b, i, j)),
        out_shape=sds((bsz, n_all, MLA_WIDTH), BF16),
        compiler_params=_params("parallel", "parallel", "parallel"),
        name="mla_attention",
    )(qa, ka, va, sg)
    mix_b = pl.pallas_call(
        functools.partial(_gqa_attn_kernel, n_all=n_all),
        grid=(bsz, pairs, nt),
        in_specs=[pl.BlockSpec((1, TM, LANES), lambda b, j, i: (b, i, j)),
                  pl.BlockSpec((1, 1, n_all, LANES), lambda b, j, i: (b, j // 2, 0, 0)),
                  pl.BlockSpec((1, 1, n_all, LANES), lambda b, j, i: (b, j // 2, 0, 0)),
                  pl.BlockSpec((1, TM, LANES), lambda b, j, i: (b, i, pairs + j))],
        out_specs=pl.BlockSpec((1, TM, LANES), lambda b, j, i: (b, i, j)),
        out_shape=sds((bsz, n_all, GQA_WIDTH), BF16),
        compiler_params=_params("parallel", "parallel", "parallel"),
        name="gqa_attention",
    )(qb, kb, vb, sg)

    off = 1 if last else 0
    tok_in = lambda w: pl.BlockSpec((1, TM, w), lambda b, i: (b, i + off, 0))
    return pl.pallas_call(
        _att_out_kernel,
        grid=(bsz, nt - off),
        in_specs=[tok_in(MLA_WIDTH), tok_in(GQA_WIDTH), full((MLA_WIDTH + GQA_WIDTH, d)), tok_in(d),
                  _mod_spec(off), full((1, d)), full((1, d))],
        out_specs=tok(d),
        out_shape=sds((bsz, n_all - off * TM, d), F32),
        compiler_params=_params("parallel", "parallel"),
        name="att_out",
    )(mix_a, mix_b, w_out.astype(BF16), xs, msel, ln_g[None, :], ln_b[None, :])


def _rec_proj_kernel(x_ref, xp_ref, xn_ref, m_ref, win_ref, cw_ref, cb_ref, gp_ref,
                     q_ref, k_ref, v_ref, bg_ref, sg_ref, xl_ref, pext_ref, *, nt):
    i = pl.program_id(1)
    shift = m_ref[0, 0, 0:1, :]
    scale = m_ref[0, 0, 1:2, :]
    mod = lambda x: x * (1.0 + scale) + shift
    prev_ok = (i >= 2).astype(F32)
    next_ok = jnp.logical_and(i >= 1, i < nt - 1).astype(F32)
    h = mod(x_ref[0])
    hext = jnp.concatenate([mod(xp_ref[0]) * prev_ok, h, mod(xn_ref[0]) * next_ok], 0)
    pext_ref[...] = _bdot(hext, win_ref[:, 0:REC_CONV_COLS])
    rest = _bdot(h, win_ref[:, REC_CONV_COLS:])
    y = cw_ref[0:1, :] * pext_ref[SUBLANES - 1:SUBLANES - 1 + TM, :]
    for j in range(1, 4):
        y = y + cw_ref[j:j + 1, :] * pext_ref[SUBLANES - 1 + j:SUBLANES - 1 + j + TM, :]
    qkv = _silu(y[:, 0:3 * GDN_WIDTH])
    xl_ref[0] = y[:, 3 * GDN_WIDTH:] + cb_ref[...]
    for hd in range(GDN_HEADS):
        sl = slice(hd * LANES, (hd + 1) * LANES)
        q = qkv[:, sl]
        k = qkv[:, GDN_WIDTH + hd * LANES:GDN_WIDTH + (hd + 1) * LANES]
        q_ref[0, :, sl] = q * (lax.rsqrt(jnp.sum(q * q, -1, keepdims=True) + EPS) * GDN_HEAD_DIM ** -0.5)
        k_ref[0, :, sl] = k * lax.rsqrt(jnp.sum(k * k, -1, keepdims=True) + EPS)
    v_ref[0] = qkv[:, 2 * GDN_WIDTH:]
    sg_ref[0] = _silu(rest[:, 0:2 * 512]).astype(BF16)
    ba = rest[:, 2 * 512:]
    t = ba + gp_ref[1:2, :]
    softplus = jnp.maximum(t, 0.0) + jnp.log(1.0 + jnp.exp(-jnp.abs(t)))
    decay = -jnp.exp(gp_ref[0:1, :]) * softplus
    bg_ref[0] = jnp.where(_lane_iota(ba.shape) < 2 * GDN_HEADS, jax.nn.sigmoid(ba), decay)


def _tri_inverse(m, strict_blocks):
    n = m.shape[0]
    eye = (lax.broadcasted_iota(jnp.int32, (n, n), 0) == lax.broadcasted_iota(jnp.int32, (n, n), 1)).astype(F32)
    md = jnp.where(strict_blocks, 0.0, m)
    l = jnp.where(strict_blocks, m, 0.0)
    p = eye - md
    pw = md
    for _ in range(3):
        pw = _bdot(pw, pw)
        p = _bdot(p, eye + pw)
    dinv = p
    e = _bdot(dinv, l)
    q = eye - e
    pw = e
    for _ in range(2):
        pw = _bdot(pw, pw)
        q = _bdot(q, eye + pw)
    return _bdot(q, dinv)


def _gdn_kernel(q_ref, k_ref, v_ref, bg_ref, o_ref, s_ref, *, rev, col):
    c = GDN_CHUNK

    @pl.when(pl.program_id(1) == 0)
    def _():
        s_ref[...] = jnp.zeros_like(s_ref)

    row_i = lax.broadcasted_iota(jnp.int32, (c, c), 0)
    col_i = lax.broadcasted_iota(jnp.int32, (c, c), 1)
    if rev:
        incl, strict = row_i <= col_i, row_i < col_i
    else:
        incl, strict = row_i >= col_i, row_i > col_i
    strict_blocks = (row_i // GDN_DIAG) != (col_i // GDN_DIAG)
    tri = incl.astype(BF16)
    last = 0 if rev else c - 1
    bg = bg_ref[0]
    for hd in range(GDN_HEADS):
        sl = slice(hd * LANES, (hd + 1) * LANES)
        q, k, v = q_ref[0, :, sl], k_ref[0, :, sl], v_ref[0, :, sl]
        beta = bg[:, col + hd:col + hd + 1]
        g = jnp.broadcast_to(bg[:, 2 * GDN_HEADS + col + hd:2 * GDN_HEADS + col + hd + 1], (c, c))
        gcum = sum(jnp.dot(tri, part, preferred_element_type=F32) for part in _split_bf16(g, 3))
        gcum_t = gcum.T
        g_last = gcum[last:last + 1, :]
        decay = jnp.exp(jnp.where(incl, gcum - gcum_t, -1e30))
        e = jnp.exp(gcum)
        kt = k.T
        kb = k * beta
        m = jnp.where(strict, _bdot(kb, kt) * decay, 0.0)
        attn = _bdot(q, kt) * decay
        t = _tri_inverse(m, strict_blocks)
        sol = _bdot(t, jnp.concatenate([v * beta, kb * e], 1))
        u, w = sol[:, :LANES], sol[:, LANES:]
        s = s_ref[hd]
        v_new = u - _bdot(w, s)
        o_ref[0, :, sl] = _bdot(q * e, s) + _bdot(attn, v_new)
        s_ref[hd] = s * jnp.exp(g_last) + _bdot(kt * jnp.exp(g_last - gcum_t), v_new)


def _lru_kernel(xl_ref, wg_ref, bgate_ref, lam_ref, o_ref, a_ref, b_ref, *, nt):
    w = xl_ref.shape[-1]
    rows = lax.broadcasted_iota(jnp.int32, (SUBLANES, w), 0)
    groups = TM // SUBLANES
    for d in range(2):
        rev = d == 1
        lam = lam_ref[d:d + 1, :]
        log_sig = jnp.minimum(lam, 0.0) - jnp.log(1.0 + jnp.exp(-jnp.abs(lam)))

        def tile_body(s, carry):
            if rev:
                t = jnp.where(s == 0, 0, nt - s)
            else:
                t = s
            base = pl.multiple_of(t * TM, TM)
            x = xl_ref[0, pl.ds(base, TM), :]
            xb = x.astype(BF16)
            r = jax.nn.sigmoid(jnp.dot(xb, wg_ref[d, 0, 0], preferred_element_type=F32) + bgate_ref[d, 0, 0])
            gi = jax.nn.sigmoid(jnp.dot(xb, wg_ref[d, 1, 0], preferred_element_type=F32) + bgate_ref[d, 1, 0])
            log_a = LRU_C * r * log_sig
            a_ref[...] = jnp.exp(log_a)
            b_ref[...] = jnp.sqrt(1.0 - jnp.exp(2.0 * log_a)) * (gi * x)

            def group_body(gidx, h_prev):
                gq = (groups - 1 - gidx) if rev else gidx
                off = pl.multiple_of(gq * SUBLANES, SUBLANES)
                a = a_ref[pl.ds(off, SUBLANES), :]
                bv = b_ref[pl.ds(off, SUBLANES), :]
                for sft in (1, 2, 4):
                    if rev:
                        ok = rows < SUBLANES - sft
                        a_s = pltpu.roll(a, SUBLANES - sft, 0)
                        b_s = pltpu.roll(bv, SUBLANES - sft, 0)
                    else:
                        ok = rows >= sft
                        a_s = pltpu.roll(a, sft, 0)
                        b_s = pltpu.roll(bv, sft, 0)
                    bv = jnp.where(ok, a * b_s + bv, bv)
                    a = jnp.where(ok, a * a_s, a)
                hcur = a * h_prev + bv
                dst = pl.ds(base + off, SUBLANES)
                if rev:
                    o_ref[0, dst, :] = o_ref[0, dst, :] + hcur
                    return jnp.broadcast_to(hcur[0:1, :], hcur.shape)
                o_ref[0, dst, :] = hcur
                return jnp.broadcast_to(hcur[SUBLANES - 1:SUBLANES, :], hcur.shape)

            return lax.fori_loop(0, groups, group_body, carry, unroll=4)

        lax.fori_loop(0, nt, tile_body, jnp.zeros((SUBLANES, w), F32))


def _rec_out_kernel(of_ref, ob_ref, r_ref, sg_ref, gn_ref, w_ref, x_ref, m_ref, g_ref, b_ref, o_ref):
    o = of_ref[0] + ob_ref[0]
    sg = sg_ref[0].astype(F32)
    y = jnp.dot((r_ref[0] * sg[:, GDN_WIDTH:]).astype(BF16), w_ref[GDN_WIDTH:, :], preferred_element_type=F32)
    for hd in range(GDN_HEADS):
        sl = slice(hd * LANES, (hd + 1) * LANES)
        oh = o[:, sl]
        og = oh * lax.rsqrt(jnp.mean(oh * oh, -1, keepdims=True) + EPS) * gn_ref[...] * sg[:, sl]
        y = y + jnp.dot(og.astype(BF16), w_ref[sl, :], preferred_element_type=F32)
    gate = m_ref[0, 0, 2:3, :]
    o_ref[0] = _layer_norm(DEEPNORM_ALPHA * x_ref[0] + gate * y, g_ref[...], b_ref[...])


def _rec_weights(w_in, gdn_conv_w, lru_conv_w, gdn_a_log, gdn_dt_bias, lru_gate_w):
    d = D_MODEL
    o = np.cumsum((0, 3 * GDN_WIDTH, GDN_WIDTH, 2 * GDN_HEADS, 2 * GDN_HEADS, LRU_WIDTH, LRU_WIDTH))
    qkv, z, b, a, xr, gr = (w_in[:, o[i]:o[i + 1]] for i in range(6))
    win = jnp.concatenate([qkv, xr, z, gr, b, a, jnp.zeros((d, LANES - 4 * GDN_HEADS), F32)], 1)
    conv_w = jnp.pad(jnp.concatenate([gdn_conv_w, lru_conv_w], 1), ((0, SUBLANES - 4), (0, 0)))
    pad = lambda p: jnp.pad(p.reshape(-1), (2 * GDN_HEADS, LANES - 4 * GDN_HEADS))
    gparams = jnp.pad(jnp.stack([pad(gdn_a_log), pad(gdn_dt_bias)], 0), ((0, SUBLANES - 2), (0, 0)))
    half = LRU_WIDTH // 2
    blocks = lru_gate_w.reshape(2, 2, 2, LRU_BLOCKS // 2, LRU_BLOCK_W, LRU_BLOCK_W)
    eye = jnp.eye(LRU_BLOCKS // 2, dtype=F32)
    wg = jnp.einsum('dghncm,nk->dghnckm', blocks, eye).reshape(2, 2, 2, half, half)
    return win.astype(BF16), conv_w, gparams, wg.astype(BF16)


def _recurrent_layer(xs, msel, w_in, w_out, gdn_conv_w, gdn_a_log, gdn_dt_bias, gdn_norm, lru_conv_w,
                     lru_conv_b, lru_gate_w, lru_gate_b, lru_lambda, ln_g, ln_b, last):
    bsz, n_all, d = xs.shape
    nt = n_all // TM
    win, conv_w, gparams, wg = _rec_weights(w_in, gdn_conv_w, lru_conv_w, gdn_a_log, gdn_dt_bias, lru_gate_w)
    full = lambda shape: pl.BlockSpec(shape, lambda b, i: (0,) * len(shape))
    tok = lambda w: pl.BlockSpec((1, TM, w), lambda b, i: (b, i, 0))
    sds = jax.ShapeDtypeStruct
    per = TM // SUBLANES
    n8 = n_all // SUBLANES
    q, k, v, bg, sg, xl = pl.pallas_call(
        functools.partial(_rec_proj_kernel, nt=nt),
        grid=(bsz, nt),
        in_specs=[tok(d),
                  pl.BlockSpec((1, SUBLANES, d), lambda b, i: (b, jnp.maximum(i * per - 1, 0), 0)),
                  pl.BlockSpec((1, SUBLANES, d), lambda b, i: (b, jnp.minimum((i + 1) * per, n8 - 1), 0)),
                  _mod_spec(), full((d, REC_COLS)), full((SUBLANES, REC_CONV_COLS)), full((1, LRU_WIDTH)),
                  full((SUBLANES, LANES))],
        out_specs=[tok(GDN_WIDTH), tok(GDN_WIDTH), tok(GDN_WIDTH), tok(LANES), tok(2 * 512), tok(LRU_WIDTH)],
        out_shape=[sds((bsz, n_all, GDN_WIDTH), F32)] * 3 + [sds((bsz, n_all, LANES), F32),
                   sds((bsz, n_all, 2 * 512), BF16), sds((bsz, n_all, LRU_WIDTH), F32)],
        scratch_shapes=[pltpu.VMEM((TM + 2 * SUBLANES, REC_CONV_COLS), F32)],
        compiler_params=_params("parallel", "parallel"),
        name="rec_proj",
    )(xs, xs, xs, msel, win, conv_w, lru_conv_b[None, :], gparams)

    nch = n_all // GDN_CHUNK
    nctx = CTX_LEN // GDN_CHUNK
    outs = []
    for dr in range(2):
        if dr == 0:
            cmap = lambda b, s: (b, s, 0)
        else:
            cmap = lambda b, s: (b, jnp.where(s < nctx, nctx - 1 - s, nch - 1 + nctx - s), 0)
        outs.append(pl.pallas_call(
            functools.partial(_gdn_kernel, rev=dr == 1, col=dr * GDN_HEADS),
            grid=(bsz, nch),
            in_specs=[pl.BlockSpec((1, GDN_CHUNK, GDN_WIDTH), cmap)] * 3 + [pl.BlockSpec((1, GDN_CHUNK, LANES), cmap)],
            out_specs=pl.BlockSpec((1, GDN_CHUNK, GDN_WIDTH), cmap),
            out_shape=sds((bsz, n_all, GDN_WIDTH), F32),
            scratch_shapes=[pltpu.VMEM((GDN_HEADS, GDN_HEAD_DIM, GDN_HEAD_DIM), F32)],
            compiler_params=_params("parallel", "arbitrary"),
            name="gdn_bwd" if dr else "gdn_fwd",
        )(q, k, v, bg))

    half = LRU_WIDTH // 2
    r = pl.pallas_call(
        functools.partial(_lru_kernel, nt=nt),
        grid=(bsz, 2),
        in_specs=[pl.BlockSpec((1, n_all, half), lambda b, c: (b, 0, c)),
                  pl.BlockSpec((2, 2, 1, half, half), lambda b, c: (0, 0, c, 0, 0)),
                  pl.BlockSpec((2, 2, 1, 1, half), lambda b, c: (0, 0, c, 0, 0)),
                  pl.BlockSpec((2, half), lambda b, c: (0, c))],
        out_specs=pl.BlockSpec((1, n_all, half), lambda b, c: (b, 0, c)),
        out_shape=sds((bsz, n_all, LRU_WIDTH), F32),
        scratch_shapes=[pltpu.VMEM((TM, half), F32), pltpu.VMEM((TM, half), F32)],
        compiler_params=_params("parallel", "parallel"),
        name="rg_lru",
    )(xl, wg, lru_gate_b.reshape(2, 2, 2, 1, half), lru_lambda)

    off = 1 if last else 0
    tok_in = lambda w: pl.BlockSpec((1, TM, w), lambda b, i: (b, i + off, 0))
    return pl.pallas_call(
        _rec_out_kernel,
        grid=(bsz, nt - off),
        in_specs=[tok_in(GDN_WIDTH), tok_in(GDN_WIDTH), tok_in(LRU_WIDTH), tok_in(2 * 512), full((1, LANES)),
                  full((GDN_WIDTH + LRU_WIDTH, d)), tok_in(d), _mod_spec(off), full((1, d)), full((1, d))],
        out_specs=tok(d),
        out_shape=sds((bsz, n_all - off * TM, d), F32),
        compiler_params=_params("parallel", "parallel"),
        name="rec_out",
    )(outs[0], outs[1], r, sg, gdn_norm[None, :], w_out.astype(BF16), xs, msel, ln_g[None, :], ln_b[None, :])


def kernel(x, c, ctx, c_ctx, mod_w, mod_b, ln_g, ln_b, att_w_in, att_w_out, mla_q_norm, mla_w_uq, mla_kv_norm, mla_w_ukv, gqa_q_norm, gqa_k_norm, rec_w_in, rec_w_out, gdn_conv_w, gdn_a_log, gdn_dt_bias, gdn_norm, lru_conv_w, lru_conv_b, lru_gate_w, lru_gate_b, lru_lambda):
    bsz, n_seq, _ = x.shape
    assert ctx.shape[1] == CTX_LEN == TM and n_seq % TM == 0 and bsz < SUBLANES
    rope_a, rope_b = _rope_tables(n_seq)
    mods = _modulation(c, c_ctx, mod_w, mod_b)
    xs = jnp.concatenate([ctx, x], 1)
    for layer in range(DEPTH):
        last = layer == DEPTH - 1
        li = layer // 2
        msel = _mod_select(mods[layer], bsz)
        if layer % 2 == 0:
            xs = _attention_layer(xs, msel, att_w_in[li], att_w_out[li], mla_q_norm[li], mla_w_uq[li],
                                  mla_kv_norm[li], mla_w_ukv[li], gqa_q_norm[li], gqa_k_norm[li],
                                  rope_a, rope_b, ln_g[layer], ln_b[layer], last)
        else:
            xs = _recurrent_layer(xs, msel, rec_w_in[li], rec_w_out[li], gdn_conv_w[li], gdn_a_log[li],
                                  gdn_dt_bias[li], gdn_norm[li], lru_conv_w[li], lru_conv_b[li],
                                  lru_gate_w[li], lru_gate_b[li], lru_lambda[li], ln_g[layer], ln_b[layer], last)
    return xs
```

```python
import functools

import numpy as np
import jax
import jax.numpy as jnp
from jax import lax
from jax.experimental import pallas as pl
from jax.experimental.pallas import tpu as pltpu

F32 = jnp.float32
BF16 = jnp.bfloat16

D_MODEL = 1024
DEPTH = 4
GRID_W = 64
CTX_LEN = 256
ROPE_THETA = 10000.0
EPS = 1e-6

MLA_HEADS = 8
MLA_Q_RANK = 256
MLA_KV_RANK = 128
MLA_NOPE = 64
MLA_ROPE = 32
MLA_V = 64
MLA_WIDTH = MLA_HEADS * MLA_V
GQA_HEADS = 8
GQA_KV_HEADS = 2
GQA_HEAD_DIM = 64
GQA_WIDTH = GQA_HEADS * GQA_HEAD_DIM
GDN_HEADS = 4
GDN_HEAD_DIM = 128
GDN_WIDTH = GDN_HEADS * GDN_HEAD_DIM
LRU_WIDTH = 512
LRU_BLOCKS = 8
LRU_BLOCK_W = LRU_WIDTH // LRU_BLOCKS
LRU_C = 8.0
DEEPNORM_ALPHA = (2 * DEPTH) ** 0.25

LANES = 128
SUBLANES = 8
TM = 256
GDN_CHUNK = 128
GDN_DIAG = 16
KEY_CHUNK = 512
LOG2_E = 1.4426950408889634
VMEM_LIMIT = 48 * 1024 * 1024

ATT_COLS = 2560
REC_CONV_COLS = 3 * GDN_WIDTH + LRU_WIDTH
REC_COLS = REC_CONV_COLS + 2 * 512 + LANES


def _params(*sem):
    return pltpu.CompilerParams(dimension_semantics=sem, vmem_limit_bytes=VMEM_LIMIT)


def _bdot(a, b):
    return jnp.dot(a.astype(BF16), b.astype(BF16), preferred_element_type=F32)


def _bdot_nt(a, b):
    return lax.dot_general(a.astype(BF16), b.astype(BF16), (((1,), (1,)), ((), ())),
                           preferred_element_type=F32)


def _split_bf16(a, parts):
    out = []
    for _ in range(parts):
        hi = a.astype(BF16)
        out.append(hi)
        a = a - hi.astype(F32)
    return out


def _dot_split(a, b):
    a_hi, a_lo = _split_bf16(a, 2)
    b_hi, b_lo = _split_bf16(b, 2)
    d = functools.partial(jnp.dot, preferred_element_type=F32)
    return d(a_hi, b_hi) + (d(a_lo, b_hi) + d(a_hi, b_lo))


def _silu(x):
    return x * jax.nn.sigmoid(x)


def _lane_iota(shape):
    return lax.broadcasted_iota(jnp.int32, shape, len(shape) - 1)


def _swap_groups(x, n):
    fwd = pltpu.roll(x, LANES - n, 1)
    bwd = pltpu.roll(x, n, 1)
    return jnp.where((_lane_iota(x.shape) % (2 * n)) < n, fwd, bwd)


def _rope(x, cos, sin, n):
    return x * cos + _swap_groups(x, n) * sin


def _layer_norm(z, g, b):
    mu = jnp.mean(z, -1, keepdims=True)
    zc = z - mu
    var = jnp.mean(zc * zc, -1, keepdims=True)
    return zc * lax.rsqrt(var + EPS) * g + b


def _mod_kernel(c_ref, w_ref, b_ref, o_ref):
    c = c_ref[...]
    o_ref[0] = _dot_split(_silu(c), w_ref[0]) + b_ref[0]


def _modulation(c, c_ctx, mod_w, mod_b):
    bsz = c.shape[0]
    rows = jnp.concatenate([c, c_ctx[None, :], jnp.zeros((SUBLANES - bsz - 1, D_MODEL), F32)], 0)
    tn = 1024
    return pl.pallas_call(
        _mod_kernel,
        grid=(DEPTH, 3 * D_MODEL // tn),
        in_specs=[pl.BlockSpec((SUBLANES, D_MODEL), lambda l, n: (0, 0)),
                  pl.BlockSpec((1, D_MODEL, tn), lambda l, n: (l, 0, n)),
                  pl.BlockSpec((1, 1, tn), lambda l, n: (l, 0, n))],
        out_specs=pl.BlockSpec((1, SUBLANES, tn), lambda l, n: (l, 0, n)),
        out_shape=jax.ShapeDtypeStruct((DEPTH, SUBLANES, 3 * D_MODEL), F32),
        compiler_params=_params("parallel", "parallel"),
        name="modulation",
    )(rows, mod_w, mod_b.reshape(DEPTH, 1, 3 * D_MODEL))


def _mod_select(mods_l, bsz):
    m3 = mods_l.reshape(SUBLANES, 3, D_MODEL)
    ctx_m = jnp.broadcast_to(m3[bsz], (bsz, 3, D_MODEL))
    sel = jnp.stack([ctx_m, m3[:bsz]], 1)
    return jnp.pad(sel, ((0, 0), (0, 0), (0, SUBLANES - 3), (0, 0)))


def _mod_spec(off=0):
    return pl.BlockSpec((1, 1, SUBLANES, D_MODEL), lambda b, i: (b, jnp.minimum(i + off, 1), 0, 0))


def _att_proj_kernel(x_ref, m_ref, win_ref, wuq_ref, wk_ref, wv_ref, nq_ref, nkv_ref, nqb_ref, nkb_ref,
                     ra_ref, rb_ref, qa_ref, ka_ref, va_ref, qb_ref, kb_ref, vb_ref, sg_ref):
    shift = m_ref[0, 0, 0:1, :]
    scale = m_ref[0, 0, 1:2, :]
    h = x_ref[0] * (1.0 + scale) + shift
    p = _bdot(h, win_ref[...])
    cq, ckv, kr = p[:, 0:256], p[:, 256:384], p[:, 384:512]
    gates = p[:, 512:1536]
    qb, kb, vb = p[:, 1536:2048], p[:, 2048:2304], p[:, 2304:2560]
    sg_ref[0] = _silu(gates).astype(BF16)

    def rms(x):
        return x * lax.rsqrt(jnp.mean(x * x, -1, keepdims=True) + EPS)

    qa = _bdot(rms(cq) * nq_ref[...], wuq_ref[...])
    ckvn = rms(ckv) * nkv_ref[...]
    kn = _bdot(ckvn, wk_ref[...])
    va_ref[0] = _bdot(ckvn, wv_ref[...]).astype(BF16)
    cos_a, sin_a = ra_ref[0], ra_ref[1]
    kr = _rope(kr, cos_a, sin_a, MLA_ROPE // 4)
    q_scale = LOG2_E * (MLA_NOPE + MLA_ROPE) ** -0.5
    for hd in range(MLA_HEADS):
        sl = slice(hd * LANES, (hd + 1) * LANES)
        qa_ref[0, hd] = (_rope(qa[:, sl], cos_a, sin_a, MLA_ROPE // 4) * q_scale).astype(BF16)
        ka_ref[0, hd] = (kn[:, sl] + kr).astype(BF16)

    cos_b, sin_b = rb_ref[0], rb_ref[1]
    low = _lane_iota((TM, LANES)) < GQA_HEAD_DIM
    for c in range(GQA_HEADS // 2):
        sl = slice(c * LANES, (c + 1) * LANES)
        xq = qb[:, sl]
        sq = xq * xq
        s_lo = jnp.sum(jnp.where(low, sq, 0.0), -1, keepdims=True)
        s_hi = jnp.sum(jnp.where(low, 0.0, sq), -1, keepdims=True)
        ms = jnp.where(low, s_lo, s_hi) * (1.0 / GQA_HEAD_DIM)
        xn = xq * lax.rsqrt(ms + EPS) * nqb_ref[:, sl]
        qb_ref[0, :, sl] = (_rope(xn, cos_b, sin_b, GQA_HEAD_DIM // 4) * (LOG2_E * GQA_HEAD_DIM ** -0.5)).astype(BF16)
    for g in range(GQA_KV_HEADS):
        sl = slice(g * LANES, (g + 1) * LANES)
        kb_ref[0, g] = _rope(rms(kb[:, sl]) * nkb_ref[...], cos_b, sin_b, GQA_HEAD_DIM // 4).astype(BF16)
        vb_ref[0, g] = vb[:, sl].astype(BF16)


def _fold_tiles(op, x, acc):
    for t in range(x.shape[1] // LANES):
        acc = op(acc, x[:, t * LANES:(t + 1) * LANES])
    return acc


def _attend_heads(qf, kf, vf, s_refs, n_heads, nk):
    n_chunks = max(nk // KEY_CHUNK, 1)
    bounds = [i * KEY_CHUNK for i in range(n_chunks)] + [nk]
    chunks = [(bounds[i], bounds[i + 1] - bounds[i]) for i in range(n_chunks)]

    def score(h, c0, n, mt):
        s = _bdot_nt(qf(h), kf(h, c0, n))
        s_refs[h % 2][:, c0:c0 + n] = s
        return _fold_tiles(jnp.maximum, s, mt)

    def consume(h, c0, n, m, lt, o):
        p = jnp.exp2(s_refs[h % 2][:, c0:c0 + n] - m)
        return _fold_tiles(jnp.add, p, lt), o + jnp.dot(p.astype(BF16), vf(h, c0, n), preferred_element_type=F32)

    neg = jnp.full((TM, LANES), -jnp.inf, F32)
    zero = jnp.zeros((TM, LANES), F32)
    mt = neg
    for c0, n in chunks:
        mt = score(0, c0, n, mt)
    outs = []
    for h in range(n_heads):
        m = jnp.max(mt, -1, keepdims=True)
        lt, o, mt = zero, zero, neg
        for c0, n in chunks:
            lt, o = consume(h, c0, n, m, lt, o)
            if h + 1 < n_heads:
                mt = score(h + 1, c0, n, mt)
        outs.append(o / jnp.sum(lt, -1, keepdims=True))
    return outs


def _store_head_pairs(outs, sg_ref, o_ref, width):
    low = _lane_iota((TM, LANES)) < width
    for j in range(len(outs) // 2):
        sl = slice(j * LANES, (j + 1) * LANES)
        o_ref[0, :, sl] = (jnp.where(low, outs[2 * j], outs[2 * j + 1]) * sg_ref[0, :, sl].astype(F32)).astype(BF16)


def _mla_attn_kernel(q_ref, k_ref, v_ref, sg_ref, o_ref, sa_ref, sb_ref, *, n_all):
    def run(nk):
        outs = _attend_heads(lambda h: q_ref[0, h], lambda h, c0, n: k_ref[0, h, c0:c0 + n, :],
                             lambda h, c0, n: v_ref[0, c0:c0 + n, (h // 2) * LANES:(h // 2 + 1) * LANES],
                             (sa_ref, sb_ref), MLA_HEADS, nk)
        _store_head_pairs(outs, sg_ref, o_ref, MLA_V)

    @pl.when(pl.program_id(1) == 0)
    def _():
        run(CTX_LEN)

    @pl.when(pl.program_id(1) > 0)
    def _():
        run(n_all)


def _gqa_attn_kernel(q_ref, k_ref, v_ref, sg_ref, o_ref, sa_ref, sb_ref, *, n_all):
    low = _lane_iota((TM, LANES)) < GQA_HEAD_DIM
    group = GQA_HEADS // GQA_KV_HEADS

    def q_head(h):
        q = q_ref[0, :, (h // 2) * LANES:(h // 2 + 1) * LANES]
        return jnp.where(low == (h % 2 == 0), q, jnp.zeros_like(q))

    def run(nk):
        outs = _attend_heads(q_head, lambda h, c0, n: k_ref[0, h // group, c0:c0 + n, :],
                             lambda h, c0, n: v_ref[0, h // group, c0:c0 + n, :],
                             (sa_ref, sb_ref), GQA_HEADS, nk)
        _store_head_pairs(outs, sg_ref, o_ref, GQA_HEAD_DIM)

    @pl.when(pl.program_id(1) == 0)
    def _():
        run(CTX_LEN)

    @pl.when(pl.program_id(1) > 0)
    def _():
        run(n_all)


def _att_out_kernel(ma_ref, mb_ref, w_ref, x_ref, m_ref, g_ref, b_ref, o_ref):
    y = (jnp.dot(ma_ref[0], w_ref[0:MLA_WIDTH, :], preferred_element_type=F32)
         + jnp.dot(mb_ref[0], w_ref[MLA_WIDTH:, :], preferred_element_type=F32))
    gate = m_ref[0, 0, 2:3, :]
    o_ref[0] = _layer_norm(DEEPNORM_ALPHA * x_ref[0] + gate * y, g_ref[...], b_ref[...])


def _rope_tables(n_seq):
    pos = jnp.arange(n_seq)
    row = (pos // GRID_W).astype(F32)[:, None]
    col = (pos % GRID_W).astype(F32)[:, None]

    def quarter(dim):
        half = dim // 2
        inv = ROPE_THETA ** (-jnp.arange(0, half, 2, dtype=F32) / half)
        return jnp.cos(row * inv), jnp.sin(row * inv), jnp.cos(col * inv), jnp.sin(col * inv)

    def with_ctx(cos, sin):
        cos = jnp.concatenate([jnp.ones((CTX_LEN, LANES), F32), cos], 0)
        sin = jnp.concatenate([jnp.zeros((CTX_LEN, LANES), F32), sin], 0)
        return jnp.stack([cos, sin], 0)

    cr, sr, cc, sc = quarter(MLA_ROPE)
    one = jnp.ones((n_seq, 1), F32)
    cos_a = jnp.concatenate([one * jnp.ones((1, MLA_NOPE)), cr, cr, cc, cc, one * jnp.ones((1, 32))], 1)
    sin_a = jnp.concatenate([one * jnp.zeros((1, MLA_NOPE)), -sr, sr, -sc, sc, one * jnp.zeros((1, 32))], 1)
    cr, sr, cc, sc = quarter(GQA_HEAD_DIM)
    cos_b = jnp.concatenate([cr, cr, cc, cc] * 2, 1)
    sin_b = jnp.concatenate([-sr, sr, -sc, sc] * 2, 1)
    return with_ctx(cos_a, sin_a), with_ctx(cos_b, sin_b)


def _att_weights(w_in, w_uq, w_ukv):
    d = D_MODEL
    o = np.cumsum((0, MLA_Q_RANK, MLA_KV_RANK, MLA_ROPE, MLA_WIDTH, GQA_WIDTH, 128, 128, GQA_WIDTH))
    cq, ckv, kr, ga, qb, kb, vb, gb = (w_in[:, o[i]:o[i + 1]] for i in range(8))
    z = lambda n: jnp.zeros((d, n), F32)
    dup = lambda w: jnp.concatenate([w[:, 0:64], w[:, 0:64], w[:, 64:128], w[:, 64:128]], 1)
    win = jnp.concatenate([cq, ckv, z(MLA_NOPE), kr, z(32), ga, gb, qb, dup(kb), dup(vb)], 1)
    wq = w_uq.reshape(MLA_Q_RANK, MLA_HEADS, MLA_NOPE + MLA_ROPE)
    wq = jnp.pad(wq, ((0, 0), (0, 0), (0, 32))).reshape(MLA_Q_RANK, MLA_HEADS * LANES)
    wkv = w_ukv.reshape(MLA_KV_RANK, MLA_HEADS, MLA_NOPE + MLA_V)
    wk = jnp.pad(wkv[:, :, :MLA_NOPE], ((0, 0), (0, 0), (0, 64))).reshape(MLA_KV_RANK, MLA_HEADS * LANES)
    wv = wkv[:, :, MLA_NOPE:].reshape(MLA_KV_RANK, MLA_WIDTH)
    return win.astype(BF16), wq.astype(BF16), wk.astype(BF16), wv.astype(BF16)


def _attention_layer(xs, msel, w_in, w_out, q_norm_a, w_uq, kv_norm_a, w_ukv, q_norm_b, k_norm_b,
                     rope_a, rope_b, ln_g, ln_b, last):
    bsz, n_all, d = xs.shape
    nt = n_all // TM
    win, wq, wk, wv = _att_weights(w_in, w_uq, w_ukv)
    full = lambda shape: pl.BlockSpec(shape, lambda b, i: (0,) * len(shape))
    tok = lambda w: pl.BlockSpec((1, TM, w), lambda b, i: (b, i, 0))
    head = lambda n: pl.BlockSpec((1, n, TM, LANES), lambda b, i: (b, 0, i, 0))
    rope = pl.BlockSpec((2, TM, LANES), lambda b, i: (0, i, 0))
    sds = jax.ShapeDtypeStruct
    qa, ka, va, qb, kb, vb, sg = pl.pallas_call(
        _att_proj_kernel,
        grid=(bsz, nt),
        in_specs=[tok(d), _mod_spec(), full((d, ATT_COLS)), full((MLA_Q_RANK, MLA_HEADS * LANES)),
                  full((MLA_KV_RANK, MLA_HEADS * LANES)), full((MLA_KV_RANK, MLA_WIDTH)),
                  full((1, MLA_Q_RANK)), full((1, MLA_KV_RANK)), full((1, GQA_WIDTH)), full((1, LANES)),
                  rope, rope],
        out_specs=[head(MLA_HEADS), head(MLA_HEADS), tok(MLA_WIDTH), tok(GQA_WIDTH),
                   head(GQA_KV_HEADS), head(GQA_KV_HEADS), tok(2 * 512)],
        out_shape=[sds((bsz, MLA_HEADS, n_all, LANES), BF16), sds((bsz, MLA_HEADS, n_all, LANES), BF16),
                   sds((bsz, n_all, MLA_WIDTH), BF16), sds((bsz, n_all, GQA_WIDTH), BF16),
                   sds((bsz, GQA_KV_HEADS, n_all, LANES), BF16), sds((bsz, GQA_KV_HEADS, n_all, LANES), BF16),
                   sds((bsz, n_all, 2 * 512), BF16)],
        compiler_params=_params("parallel", "parallel"),
        name="att_proj",
    )(xs, msel, win, wq, wk, wv, q_norm_a[None, :], kv_norm_a[None, :],
      jnp.tile(q_norm_b, GQA_HEADS)[None, :], jnp.tile(k_norm_b, 2)[None, :], rope_a, rope_b)

    whole = lambda n: pl.BlockSpec((1, n, n_all, LANES), lambda b, i: (b, 0, 0, 0))
    mix_a = pl.pallas_call(
        functools.partial(_mla_attn_kernel, n_all=n_all),
        grid=(bsz, nt),
        in_specs=[head(MLA_HEADS), whole(MLA_HEADS), pl.BlockSpec((1, n_all, MLA_WIDTH), lambda b, i: (b, 0, 0)),
                  tok(MLA_WIDTH)],
        out_specs=tok(MLA_WIDTH),
        out_shape=sds((bsz, n_all, MLA_WIDTH), BF16),
        scratch_shapes=[pltpu.VMEM((TM, n_all), F32)] * 2,
        compiler_params=_params("parallel", "parallel"),
        name="mla_attention",
    )(qa, ka, va, sg)
    mix_b = pl.pallas_call(
        functools.partial(_gqa_attn_kernel, n_all=n_all),
        grid=(bsz, nt),
        in_specs=[tok(GQA_WIDTH), whole(GQA_KV_HEADS), whole(GQA_KV_HEADS),
                  pl.BlockSpec((1, TM, GQA_WIDTH), lambda b, i: (b, i, 1))],
        out_specs=tok(GQA_WIDTH),
        out_shape=sds((bsz, n_all, GQA_WIDTH), BF16),
        scratch_shapes=[pltpu.VMEM((TM, n_all), F32)] * 2,
        compiler_params=_params("parallel", "parallel"),
        name="gqa_attention",
    )(qb, kb, vb, sg)

    off = 1 if last else 0
    tok_in = lambda w: pl.BlockSpec((1, TM, w), lambda b, i: (b, i + off, 0))
    return pl.pallas_call(
        _att_out_kernel,
        grid=(bsz, nt - off),
        in_specs=[tok_in(MLA_WIDTH), tok_in(GQA_WIDTH), full((MLA_WIDTH + GQA_WIDTH, d)), tok_in(d),
                  _mod_spec(off), full((1, d)), full((1, d))],
        out_specs=tok(d),
        out_shape=sds((bsz, n_all - off * TM, d), F32),
        compiler_params=_params("parallel", "parallel"),
        name="att_out",
    )(mix_a, mix_b, w_out.astype(BF16), xs, msel, ln_g[None, :], ln_b[None, :])


def _rec_proj_kernel(x_ref, xp_ref, xn_ref, m_ref, win_ref, cw_ref, cb_ref, gp_ref,
                     q_ref, k_ref, v_ref, bg_ref, sg_ref, xl_ref, pext_ref, *, nt):
    i = pl.program_id(1)
    shift = m_ref[0, 0, 0:1, :]
    scale = m_ref[0, 0, 1:2, :]
    mod = lambda x: x * (1.0 + scale) + shift
    prev_ok = (i >= 2).astype(F32)
    next_ok = jnp.logical_and(i >= 1, i < nt - 1).astype(F32)
    h = mod(x_ref[0])
    hext = jnp.concatenate([mod(xp_ref[0]) * prev_ok, h, mod(xn_ref[0]) * next_ok], 0)
    pext_ref[...] = _bdot(hext, win_ref[:, 0:REC_CONV_COLS])
    rest = _bdot(h, win_ref[:, REC_CONV_COLS:])
    y = cw_ref[0:1, :] * pext_ref[SUBLANES - 1:SUBLANES - 1 + TM, :]
    for j in range(1, 4):
        y = y + cw_ref[j:j + 1, :] * pext_ref[SUBLANES - 1 + j:SUBLANES - 1 + j + TM, :]
    qkv = _silu(y[:, 0:3 * GDN_WIDTH])
    xl_ref[0] = y[:, 3 * GDN_WIDTH:] + cb_ref[...]
    for hd in range(GDN_HEADS):
        sl = slice(hd * LANES, (hd + 1) * LANES)
        q = qkv[:, sl]
        k = qkv[:, GDN_WIDTH + hd * LANES:GDN_WIDTH + (hd + 1) * LANES]
        q_ref[0, :, sl] = q * (lax.rsqrt(jnp.sum(q * q, -1, keepdims=True) + EPS) * GDN_HEAD_DIM ** -0.5)
        k_ref[0, :, sl] = k * lax.rsqrt(jnp.sum(k * k, -1, keepdims=True) + EPS)
    v_ref[0] = qkv[:, 2 * GDN_WIDTH:]
    sg_ref[0] = _silu(rest[:, 0:2 * 512]).astype(BF16)
    ba = rest[:, 2 * 512:]
    t = ba + gp_ref[1:2, :]
    softplus = jnp.maximum(t, 0.0) + jnp.log(1.0 + jnp.exp(-jnp.abs(t)))
    decay = -jnp.exp(gp_ref[0:1, :]) * softplus
    bg_ref[0] = jnp.where(_lane_iota(ba.shape) < 2 * GDN_HEADS, jax.nn.sigmoid(ba), decay)


def _each(f, *lists):
    return [f(*args) for args in zip(*lists)]


def _tri_inverse(ms, strict_blocks, eye):
    mds = [jnp.where(strict_blocks, 0.0, m) for m in ms]
    ls = [jnp.where(strict_blocks, m, 0.0) for m in ms]
    ps = [eye - md for md in mds]
    pws = mds
    for _ in range(3):
        pws = _each(_bdot, pws, pws)
        ps = _each(lambda p, pw: _bdot(p, eye + pw), ps, pws)
    dinvs = ps
    es = _each(_bdot, dinvs, ls)
    qs = [eye - e for e in es]
    pws = es
    for _ in range(2):
        pws = _each(_bdot, pws, pws)
        qs = _each(lambda q, pw: _bdot(q, eye + pw), qs, pws)
    return _each(_bdot, qs, dinvs)


def _gdn_kernel(*refs):
    c = GDN_CHUNK
    ins, outs, s_ref = (refs[0:4], refs[4:8]), refs[8:10], refs[10]

    @pl.when(pl.program_id(1) == 0)
    def _():
        s_ref[...] = jnp.zeros_like(s_ref)

    row_i = lax.broadcasted_iota(jnp.int32, (c, c), 0)
    col_i = lax.broadcasted_iota(jnp.int32, (c, c), 1)
    eye = (row_i == col_i).astype(F32)
    strict_blocks = (row_i // GDN_DIAG) != (col_i // GDN_DIAG)
    incls = [row_i >= col_i, row_i <= col_i]
    stricts = [row_i > col_i, row_i < col_i]
    lasts = [c - 1, 0]
    streams = [(d, hd) for d in range(2) for hd in range(GDN_HEADS)]
    lanes = lambda hd: slice(hd * LANES, (hd + 1) * LANES)
    bgs = [ins[d][3][0] for d in range(2)]

    qs = [ins[d][0][0, :, lanes(hd)] for d, hd in streams]
    ks = [ins[d][1][0, :, lanes(hd)] for d, hd in streams]
    vs = [ins[d][2][0, :, lanes(hd)] for d, hd in streams]
    betas = [bgs[d][:, d * GDN_HEADS + hd:d * GDN_HEADS + hd + 1] for d, hd in streams]
    gs = [jnp.broadcast_to(bgs[d][:, (2 + d) * GDN_HEADS + hd:(2 + d) * GDN_HEADS + hd + 1], (c, c))
          for d, hd in streams]
    tris = [incls[d].astype(BF16) for d, _ in streams]
    parts = [_split_bf16(g, 3) for g in gs]
    gcums = [sum(jnp.dot(tri, p, preferred_element_type=F32) for p in ps) for tri, ps in zip(tris, parts)]
    gcum_ts = [g.T for g in gcums]
    g_lasts = [g[lasts[d]:lasts[d] + 1, :] for g, (d, _) in zip(gcums, streams)]
    decays = [jnp.exp(jnp.where(incls[d], g - gt, -1e30)) for g, gt, (d, _) in zip(gcums, gcum_ts, streams)]
    es = [jnp.exp(g) for g in gcums]
    kts = [k.T for k in ks]
    kbs = _each(lambda k, b: k * b, ks, betas)
    kks = _each(_bdot, kbs, kts)
    qks = _each(_bdot, qs, kts)
    ms = [jnp.where(stricts[d], kk * dec, 0.0) for kk, dec, (d, _) in zip(kks, decays, streams)]
    attns = _each(lambda a, dec: a * dec, qks, decays)
    ts = _tri_inverse(ms, strict_blocks, eye)
    sols = _each(lambda t, v, b, kb, e: _bdot(t, jnp.concatenate([v * b, kb * e], 1)), ts, vs, betas, kbs, es)
    ss = [s_ref[d, hd] for d, hd in streams]
    v_news = _each(lambda sol, s: sol[:, :LANES] - _bdot(sol[:, LANES:], s), sols, ss)
    kdecs = _each(lambda kt, gl, gt: kt * jnp.exp(gl - gt), kts, g_lasts, gcum_ts)
    s_upds = _each(_bdot, kdecs, v_news)
    o_states = _each(lambda q, e, s: _bdot(q * e, s), qs, es, ss)
    o_locals = _each(_bdot, attns, v_news)
    for (d, hd), s, gl, upd, o1, o2 in zip(streams, ss, g_lasts, s_upds, o_states, o_locals):
        s_ref[d, hd] = s * jnp.exp(gl) + upd
        outs[d][0, :, lanes(hd)] = o1 + o2


def _lru_kernel(xl_ref, wg_ref, bgate_ref, lam_ref, o_ref, a_ref, b_ref, *, nt):
    w = xl_ref.shape[-1]
    rows = lax.broadcasted_iota(jnp.int32, (SUBLANES, w), 0)
    groups = TM // SUBLANES
    for d in range(2):
        rev = d == 1
        lam = lam_ref[d:d + 1, :]
        log_sig = jnp.minimum(lam, 0.0) - jnp.log(1.0 + jnp.exp(-jnp.abs(lam)))

        def tile_body(s, carry):
            if rev:
                t = jnp.where(s == 0, 0, nt - s)
            else:
                t = s
            base = pl.multiple_of(t * TM, TM)
            x = xl_ref[0, pl.ds(base, TM), :]
            xb = x.astype(BF16)
            r = jax.nn.sigmoid(jnp.dot(xb, wg_ref[d, 0, 0], preferred_element_type=F32) + bgate_ref[d, 0, 0])
            gi = jax.nn.sigmoid(jnp.dot(xb, wg_ref[d, 1, 0], preferred_element_type=F32) + bgate_ref[d, 1, 0])
            log_a = LRU_C * r * log_sig
            a_ref[...] = jnp.exp(log_a)
            b_ref[...] = jnp.sqrt(1.0 - jnp.exp(2.0 * log_a)) * (gi * x)

            def group_body(gidx, h_prev):
                gq = (groups - 1 - gidx) if rev else gidx
                off = pl.multiple_of(gq * SUBLANES, SUBLANES)
                a = a_ref[pl.ds(off, SUBLANES), :]
                bv = b_ref[pl.ds(off, SUBLANES), :]
                for sft in (1, 2, 4):
                    if rev:
                        ok = rows < SUBLANES - sft
                        a_s = pltpu.roll(a, SUBLANES - sft, 0)
                        b_s = pltpu.roll(bv, SUBLANES - sft, 0)
                    else:
                        ok = rows >= sft
                        a_s = pltpu.roll(a, sft, 0)
                        b_s = pltpu.roll(bv, sft, 0)
                    bv = jnp.where(ok, a * b_s + bv, bv)
                    a = jnp.where(ok, a * a_s, a)
                hcur = a * h_prev + bv
                dst = pl.ds(base + off, SUBLANES)
                if rev:
                    o_ref[0, dst, :] = o_ref[0, dst, :] + hcur
                    return jnp.broadcast_to(hcur[0:1, :], hcur.shape)
                o_ref[0, dst, :] = hcur
                return jnp.broadcast_to(hcur[SUBLANES - 1:SUBLANES, :], hcur.shape)

            return lax.fori_loop(0, groups, group_body, carry, unroll=4)

        lax.fori_loop(0, nt, tile_body, jnp.zeros((SUBLANES, w), F32))


def _rec_out_kernel(of_ref, ob_ref, r_ref, sg_ref, gn_ref, w_ref, x_ref, m_ref, g_ref, b_ref, o_ref):
    o = of_ref[0] + ob_ref[0]
    sg = sg_ref[0].astype(F32)
    y = jnp.dot((r_ref[0] * sg[:, GDN_WIDTH:]).astype(BF16), w_ref[GDN_WIDTH:, :], preferred_element_type=F32)
    for hd in range(GDN_HEADS):
        sl = slice(hd * LANES, (hd + 1) * LANES)
        oh = o[:, sl]
        og = oh * lax.rsqrt(jnp.mean(oh * oh, -1, keepdims=True) + EPS) * gn_ref[...] * sg[:, sl]
        y = y + jnp.dot(og.astype(BF16), w_ref[sl, :], preferred_element_type=F32)
    gate = m_ref[0, 0, 2:3, :]
    o_ref[0] = _layer_norm(DEEPNORM_ALPHA * x_ref[0] + gate * y, g_ref[...], b_ref[...])


def _rec_weights(w_in, gdn_conv_w, lru_conv_w, gdn_a_log, gdn_dt_bias, lru_gate_w):
    d = D_MODEL
    o = np.cumsum((0, 3 * GDN_WIDTH, GDN_WIDTH, 2 * GDN_HEADS, 2 * GDN_HEADS, LRU_WIDTH, LRU_WIDTH))
    qkv, z, b, a, xr, gr = (w_in[:, o[i]:o[i + 1]] for i in range(6))
    win = jnp.concatenate([qkv, xr, z, gr, b, a, jnp.zeros((d, LANES - 4 * GDN_HEADS), F32)], 1)
    conv_w = jnp.pad(jnp.concatenate([gdn_conv_w, lru_conv_w], 1), ((0, SUBLANES - 4), (0, 0)))
    pad = lambda p: jnp.pad(p.reshape(-1), (2 * GDN_HEADS, LANES - 4 * GDN_HEADS))
    gparams = jnp.pad(jnp.stack([pad(gdn_a_log), pad(gdn_dt_bias)], 0), ((0, SUBLANES - 2), (0, 0)))
    half = LRU_WIDTH // 2
    blocks = lru_gate_w.reshape(2, 2, 2, LRU_BLOCKS // 2, LRU_BLOCK_W, LRU_BLOCK_W)
    eye = jnp.eye(LRU_BLOCKS // 2, dtype=F32)
    wg = jnp.einsum('dghncm,nk->dghnckm', blocks, eye).reshape(2, 2, 2, half, half)
    return win.astype(BF16), conv_w, gparams, wg.astype(BF16)


def _recurrent_layer(xs, msel, w_in, w_out, gdn_conv_w, gdn_a_log, gdn_dt_bias, gdn_norm, lru_conv_w,
                     lru_conv_b, lru_gate_w, lru_gate_b, lru_lambda, ln_g, ln_b, last):
    bsz, n_all, d = xs.shape
    nt = n_all // TM
    win, conv_w, gparams, wg = _rec_weights(w_in, gdn_conv_w, lru_conv_w, gdn_a_log, gdn_dt_bias, lru_gate_w)
    full = lambda shape: pl.BlockSpec(shape, lambda b, i: (0,) * len(shape))
    tok = lambda w: pl.BlockSpec((1, TM, w), lambda b, i: (b, i, 0))
    sds = jax.ShapeDtypeStruct
    per = TM // SUBLANES
    n8 = n_all // SUBLANES
    q, k, v, bg, sg, xl = pl.pallas_call(
        functools.partial(_rec_proj_kernel, nt=nt),
        grid=(bsz, nt),
        in_specs=[tok(d),
                  pl.BlockSpec((1, SUBLANES, d), lambda b, i: (b, jnp.maximum(i * per - 1, 0), 0)),
                  pl.BlockSpec((1, SUBLANES, d), lambda b, i: (b, jnp.minimum((i + 1) * per, n8 - 1), 0)),
                  _mod_spec(), full((d, REC_COLS)), full((SUBLANES, REC_CONV_COLS)), full((1, LRU_WIDTH)),
                  full((SUBLANES, LANES))],
        out_specs=[tok(GDN_WIDTH), tok(GDN_WIDTH), tok(GDN_WIDTH), tok(LANES), tok(2 * 512), tok(LRU_WIDTH)],
        out_shape=[sds((bsz, n_all, GDN_WIDTH), F32)] * 3 + [sds((bsz, n_all, LANES), F32),
                   sds((bsz, n_all, 2 * 512), BF16), sds((bsz, n_all, LRU_WIDTH), F32)],
        scratch_shapes=[pltpu.VMEM((TM + 2 * SUBLANES, REC_CONV_COLS), F32)],
        compiler_params=_params("parallel", "parallel"),
        name="rec_proj",
    )(xs, xs, xs, msel, win, conv_w, lru_conv_b[None, :], gparams)

    nch = n_all // GDN_CHUNK
    nctx = CTX_LEN // GDN_CHUNK
    fmap = lambda b, s: (b, s, 0)
    bmap = lambda b, s: (b, jnp.where(s < nctx, nctx - 1 - s, nch - 1 + nctx - s), 0)
    chunk_specs = lambda cmap: [pl.BlockSpec((1, GDN_CHUNK, GDN_WIDTH), cmap)] * 3 + [pl.BlockSpec((1, GDN_CHUNK, LANES), cmap)]
    outs = pl.pallas_call(
        _gdn_kernel,
        grid=(bsz, nch),
        in_specs=chunk_specs(fmap) + chunk_specs(bmap),
        out_specs=[pl.BlockSpec((1, GDN_CHUNK, GDN_WIDTH), fmap), pl.BlockSpec((1, GDN_CHUNK, GDN_WIDTH), bmap)],
        out_shape=[sds((bsz, n_all, GDN_WIDTH), F32)] * 2,
        scratch_shapes=[pltpu.VMEM((2, GDN_HEADS, GDN_HEAD_DIM, GDN_HEAD_DIM), F32)],
        compiler_params=_params("parallel", "arbitrary"),
        name="gdn",
    )(q, k, v, bg, q, k, v, bg)

    half = LRU_WIDTH // 2
    r = pl.pallas_call(
        functools.partial(_lru_kernel, nt=nt),
        grid=(bsz, 2),
        in_specs=[pl.BlockSpec((1, n_all, half), lambda b, c: (b, 0, c)),
                  pl.BlockSpec((2, 2, 1, half, half), lambda b, c: (0, 0, c, 0, 0)),
                  pl.BlockSpec((2, 2, 1, 1, half), lambda b, c: (0, 0, c, 0, 0)),
                  pl.BlockSpec((2, half), lambda b, c: (0, c))],
        out_specs=pl.BlockSpec((1, n_all, half), lambda b, c: (b, 0, c)),
        out_shape=sds((bsz, n_all, LRU_WIDTH), F32),
        scratch_shapes=[pltpu.VMEM((TM, half), F32), pltpu.VMEM((TM, half), F32)],
        compiler_params=_params("parallel", "parallel"),
        name="rg_lru",
    )(xl, wg, lru_gate_b.reshape(2, 2, 2, 1, half), lru_lambda)

    off = 1 if last else 0
    tok_in = lambda w: pl.BlockSpec((1, TM, w), lambda b, i: (b, i + off, 0))
    return pl.pallas_call(
        _rec_out_kernel,
        grid=(bsz, nt - off),
        in_specs=[tok_in(GDN_WIDTH), tok_in(GDN_WIDTH), tok_in(LRU_WIDTH), tok_in(2 * 512), full((1, LANES)),
                  full((GDN_WIDTH + LRU_WIDTH, d)), tok_in(d), _mod_spec(off), full((1, d)), full((1, d))],
        out_specs=tok(d),
        out_shape=sds((bsz, n_all - off * TM, d), F32),
        compiler_params=_params("parallel", "parallel"),
        name="rec_out",
    )(outs[0], outs[1], r, sg, gdn_norm[None, :], w_out.astype(BF16), xs, msel, ln_g[None, :], ln_b[None, :])


def kernel(x, c, ctx, c_ctx, mod_w, mod_b, ln_g, ln_b, att_w_in, att_w_out, mla_q_norm, mla_w_uq, mla_kv_norm, mla_w_ukv, gqa_q_norm, gqa_k_norm, rec_w_in, rec_w_out, gdn_conv_w, gdn_a_log, gdn_dt_bias, gdn_norm, lru_conv_w, lru_conv_b, lru_gate_w, lru_gate_b, lru_lambda):
    bsz, n_seq, _ = x.shape
    assert ctx.shape[1] == CTX_LEN == TM and n_seq % TM == 0 and bsz < SUBLANES
    rope_a, rope_b = _rope_tables(n_seq)
    mods = _modulation(c, c_ctx, mod_w, mod_b)
    xs = jnp.concatenate([ctx, x], 1)
    for layer in range(DEPTH):
        last = layer == DEPTH - 1
        li = layer // 2
        msel = _mod_select(mods[layer], bsz)
        if layer % 2 == 0:
            xs = _attention_layer(xs, msel, att_w_in[li], att_w_out[li], mla_q_norm[li], mla_w_uq[li],
                                  mla_kv_norm[li], mla_w_ukv[li], gqa_q_norm[li], gqa_k_norm[li],
                                  rope_a, rope_b, ln_g[layer], ln_b[layer], last)
        else:
            xs = _recurrent_layer(xs, msel, rec_w_in[li], rec_w_out[li], gdn_conv_w[li], gdn_a_log[li],
                                  gdn_dt_bias[li], gdn_norm[li], lru_conv_w[li], lru_conv_b[li],
                                  lru_gate_w[li], lru_gate_b[li], lru_lambda[li], ln_g[layer], ln_b[layer], last)
    return xs
```

```python
import functools

import numpy as np
import jax
import jax.numpy as jnp
from jax import lax
from jax.experimental import pallas as pl
from jax.experimental.pallas import tpu as pltpu

F32 = jnp.float32
BF16 = jnp.bfloat16

D_MODEL = 1024
DEPTH = 4
GRID_W = 64
CTX_LEN = 256
ROPE_THETA = 10000.0
EPS = 1e-6

MLA_HEADS = 8
MLA_Q_RANK = 256
MLA_KV_RANK = 128
MLA_NOPE = 64
MLA_ROPE = 32
MLA_V = 64
MLA_WIDTH = MLA_HEADS * MLA_V
GQA_HEADS = 8
GQA_KV_HEADS = 2
GQA_HEAD_DIM = 64
GQA_WIDTH = GQA_HEADS * GQA_HEAD_DIM
GDN_HEADS = 4
GDN_HEAD_DIM = 128
GDN_WIDTH = GDN_HEADS * GDN_HEAD_DIM
LRU_WIDTH = 512
LRU_BLOCKS = 8
LRU_BLOCK_W = LRU_WIDTH // LRU_BLOCKS
LRU_C = 8.0
DEEPNORM_ALPHA = (2 * DEPTH) ** 0.25

LANES = 128
SUBLANES = 8
TM = 256
GDN_CHUNK = 128
GDN_DIAG = 16
KEY_CHUNK = 512
LOG2_E = 1.4426950408889634
VMEM_LIMIT = 48 * 1024 * 1024

ATT_COLS = 2560
REC_CONV_COLS = 3 * GDN_WIDTH + LRU_WIDTH
REC_COLS = REC_CONV_COLS + 2 * 512 + LANES


def _params(*sem):
    return pltpu.CompilerParams(dimension_semantics=sem, vmem_limit_bytes=VMEM_LIMIT)


def _bdot(a, b):
    return jnp.dot(a.astype(BF16), b.astype(BF16), preferred_element_type=F32)


def _bdot_nt(a, b):
    return lax.dot_general(a.astype(BF16), b.astype(BF16), (((1,), (1,)), ((), ())),
                           preferred_element_type=F32)


def _split_bf16(a, parts):
    out = []
    for _ in range(parts):
        hi = a.astype(BF16)
        out.append(hi)
        a = a - hi.astype(F32)
    return out


def _dot_split(a, b):
    a_hi, a_lo = _split_bf16(a, 2)
    b_hi, b_lo = _split_bf16(b, 2)
    d = functools.partial(jnp.dot, preferred_element_type=F32)
    return d(a_hi, b_hi) + (d(a_lo, b_hi) + d(a_hi, b_lo))


def _silu(x):
    return x * jax.nn.sigmoid(x)


def _lane_iota(shape):
    return lax.broadcasted_iota(jnp.int32, shape, len(shape) - 1)


def _swap_groups(x, n):
    fwd = pltpu.roll(x, LANES - n, 1)
    bwd = pltpu.roll(x, n, 1)
    return jnp.where((_lane_iota(x.shape) % (2 * n)) < n, fwd, bwd)


def _rope(x, cos, sin, n):
    return x * cos + _swap_groups(x, n) * sin


def _layer_norm(z, g, b):
    mu = jnp.mean(z, -1, keepdims=True)
    zc = z - mu
    var = jnp.mean(zc * zc, -1, keepdims=True)
    return zc * lax.rsqrt(var + EPS) * g + b


def _mod_kernel(c_ref, w_ref, b_ref, o_ref):
    c = c_ref[...]
    o_ref[0] = _dot_split(_silu(c), w_ref[0]) + b_ref[0]


def _modulation(c, c_ctx, mod_w, mod_b):
    bsz = c.shape[0]
    rows = jnp.concatenate([c, c_ctx[None, :], jnp.zeros((SUBLANES - bsz - 1, D_MODEL), F32)], 0)
    tn = 1024
    return pl.pallas_call(
        _mod_kernel,
        grid=(DEPTH, 3 * D_MODEL // tn),
        in_specs=[pl.BlockSpec((SUBLANES, D_MODEL), lambda l, n: (0, 0)),
                  pl.BlockSpec((1, D_MODEL, tn), lambda l, n: (l, 0, n)),
                  pl.BlockSpec((1, 1, tn), lambda l, n: (l, 0, n))],
        out_specs=pl.BlockSpec((1, SUBLANES, tn), lambda l, n: (l, 0, n)),
        out_shape=jax.ShapeDtypeStruct((DEPTH, SUBLANES, 3 * D_MODEL), F32),
        compiler_params=_params("parallel", "parallel"),
        name="modulation",
    )(rows, mod_w, mod_b.reshape(DEPTH, 1, 3 * D_MODEL))


def _mod_select(mods_l, bsz):
    m3 = mods_l.reshape(SUBLANES, 3, D_MODEL)
    ctx_m = jnp.broadcast_to(m3[bsz], (bsz, 3, D_MODEL))
    sel = jnp.stack([ctx_m, m3[:bsz]], 1)
    return jnp.pad(sel, ((0, 0), (0, 0), (0, SUBLANES - 3), (0, 0)))


def _mod_spec(off=0):
    return pl.BlockSpec((1, 1, SUBLANES, D_MODEL), lambda b, i: (b, jnp.minimum(i + off, 1), 0, 0))


def _att_proj_kernel(x_ref, m_ref, win_ref, wuq_ref, wk_ref, wv_ref, nq_ref, nkv_ref, nqb_ref, nkb_ref,
                     ra_ref, rb_ref, qa_ref, ka_ref, va_ref, qb_ref, kb_ref, vb_ref, sg_ref):
    shift = m_ref[0, 0, 0:1, :]
    scale = m_ref[0, 0, 1:2, :]
    h = x_ref[0] * (1.0 + scale) + shift
    p = _bdot(h, win_ref[...])
    cq, ckv, kr = p[:, 0:256], p[:, 256:384], p[:, 384:512]
    gates = p[:, 512:1536]
    qb, kb, vb = p[:, 1536:2048], p[:, 2048:2304], p[:, 2304:2560]
    sg_ref[0] = _silu(gates).astype(BF16)

    def rms(x):
        return x * lax.rsqrt(jnp.mean(x * x, -1, keepdims=True) + EPS)

    qa = _bdot(rms(cq) * nq_ref[...], wuq_ref[...])
    ckvn = rms(ckv) * nkv_ref[...]
    kn = _bdot(ckvn, wk_ref[...])
    va_ref[0] = _bdot(ckvn, wv_ref[...]).astype(BF16)
    cos_a, sin_a = ra_ref[0], ra_ref[1]
    kr = _rope(kr, cos_a, sin_a, MLA_ROPE // 4)
    q_scale = LOG2_E * (MLA_NOPE + MLA_ROPE) ** -0.5
    for hd in range(MLA_HEADS):
        sl = slice(hd * LANES, (hd + 1) * LANES)
        qa_ref[0, hd] = (_rope(qa[:, sl], cos_a, sin_a, MLA_ROPE // 4) * q_scale).astype(BF16)
        ka_ref[0, hd] = (kn[:, sl] + kr).astype(BF16)

    cos_b, sin_b = rb_ref[0], rb_ref[1]
    low = _lane_iota((TM, LANES)) < GQA_HEAD_DIM
    for c in range(GQA_HEADS // 2):
        sl = slice(c * LANES, (c + 1) * LANES)
        xq = qb[:, sl]
        sq = xq * xq
        s_lo = jnp.sum(jnp.where(low, sq, 0.0), -1, keepdims=True)
        s_hi = jnp.sum(jnp.where(low, 0.0, sq), -1, keepdims=True)
        ms = jnp.where(low, s_lo, s_hi) * (1.0 / GQA_HEAD_DIM)
        xn = xq * lax.rsqrt(ms + EPS) * nqb_ref[:, sl]
        qb_ref[0, :, sl] = (_rope(xn, cos_b, sin_b, GQA_HEAD_DIM // 4) * (LOG2_E * GQA_HEAD_DIM ** -0.5)).astype(BF16)
    for g in range(GQA_KV_HEADS):
        sl = slice(g * LANES, (g + 1) * LANES)
        kb_ref[0, g] = _rope(rms(kb[:, sl]) * nkb_ref[...], cos_b, sin_b, GQA_HEAD_DIM // 4).astype(BF16)
        vb_ref[0, g] = vb[:, sl].astype(BF16)


def _fold_tiles(op, x, acc):
    for t in range(x.shape[1] // LANES):
        acc = op(acc, x[:, t * LANES:(t + 1) * LANES])
    return acc


def _attend_heads(qf, kf, vf, s_refs, n_heads, nk):
    n_chunks = max(nk // KEY_CHUNK, 1)
    bounds = [i * KEY_CHUNK for i in range(n_chunks)] + [nk]
    chunks = [(bounds[i], bounds[i + 1] - bounds[i]) for i in range(n_chunks)]

    def score(h, c0, n, mt):
        s = _bdot_nt(qf(h), kf(h, c0, n))
        s_refs[h % 2][:, c0:c0 + n] = s
        return _fold_tiles(jnp.maximum, s, mt)

    def consume(h, c0, n, m, lt, o):
        p = jnp.exp2(s_refs[h % 2][:, c0:c0 + n] - m)
        return _fold_tiles(jnp.add, p, lt), o + jnp.dot(p.astype(BF16), vf(h, c0, n), preferred_element_type=F32)

    neg = jnp.full((TM, LANES), -jnp.inf, F32)
    zero = jnp.zeros((TM, LANES), F32)
    mt = neg
    for c0, n in chunks:
        mt = score(0, c0, n, mt)
    outs = []
    for h in range(n_heads):
        m = jnp.max(mt, -1, keepdims=True)
        lt, o, mt = zero, zero, neg
        for c0, n in chunks:
            lt, o = consume(h, c0, n, m, lt, o)
            if h + 1 < n_heads:
                mt = score(h + 1, c0, n, mt)
        outs.append(o / jnp.sum(lt, -1, keepdims=True))
    return outs


def _store_head_pairs(outs, sg_ref, o_ref, width):
    low = _lane_iota((TM, LANES)) < width
    for j in range(len(outs) // 2):
        sl = slice(j * LANES, (j + 1) * LANES)
        o_ref[0, :, sl] = (jnp.where(low, outs[2 * j], outs[2 * j + 1]) * sg_ref[0, :, sl].astype(F32)).astype(BF16)


def _mla_attn_kernel(q_ref, k_ref, v_ref, sg_ref, o_ref, sa_ref, sb_ref, *, n_all):
    def run(nk):
        outs = _attend_heads(lambda h: q_ref[0, h], lambda h, c0, n: k_ref[0, h, c0:c0 + n, :],
                             lambda h, c0, n: v_ref[0, c0:c0 + n, (h // 2) * LANES:(h // 2 + 1) * LANES],
                             (sa_ref, sb_ref), MLA_HEADS, nk)
        _store_head_pairs(outs, sg_ref, o_ref, MLA_V)

    @pl.when(pl.program_id(1) == 0)
    def _():
        run(CTX_LEN)

    @pl.when(pl.program_id(1) > 0)
    def _():
        run(n_all)


def _gqa_attn_kernel(q_ref, k_ref, v_ref, sg_ref, o_ref, sa_ref, sb_ref, *, n_all):
    low = _lane_iota((TM, LANES)) < GQA_HEAD_DIM
    group = GQA_HEADS // GQA_KV_HEADS

    def q_head(h):
        q = q_ref[0, :, (h // 2) * LANES:(h // 2 + 1) * LANES]
        return jnp.where(low == (h % 2 == 0), q, jnp.zeros_like(q))

    def run(nk):
        outs = _attend_heads(q_head, lambda h, c0, n: k_ref[0, h // group, c0:c0 + n, :],
                             lambda h, c0, n: v_ref[0, h // group, c0:c0 + n, :],
                             (sa_ref, sb_ref), GQA_HEADS, nk)
        _store_head_pairs(outs, sg_ref, o_ref, GQA_HEAD_DIM)

    @pl.when(pl.program_id(1) == 0)
    def _():
        run(CTX_LEN)

    @pl.when(pl.program_id(1) > 0)
    def _():
        run(n_all)


def _att_out_kernel(ma_ref, mb_ref, w_ref, x_ref, m_ref, g_ref, b_ref, o_ref):
    y = (jnp.dot(ma_ref[0], w_ref[0:MLA_WIDTH, :], preferred_element_type=F32)
         + jnp.dot(mb_ref[0], w_ref[MLA_WIDTH:, :], preferred_element_type=F32))
    gate = m_ref[0, 0, 2:3, :]
    o_ref[0] = _layer_norm(DEEPNORM_ALPHA * x_ref[0] + gate * y, g_ref[...], b_ref[...])


def _rope_tables(n_seq):
    pos = np.arange(n_seq)
    row = (pos // GRID_W).astype(np.float32)[:, None]
    col = (pos % GRID_W).astype(np.float32)[:, None]

    def quarter(dim):
        half = dim // 2
        inv = (ROPE_THETA ** (-np.arange(0, half, 2, dtype=np.float32) / half)).astype(np.float32)
        return np.cos(row * inv), np.sin(row * inv), np.cos(col * inv), np.sin(col * inv)

    def with_ctx(cos, sin):
        cos = np.concatenate([np.ones((CTX_LEN, LANES), np.float32), cos], 0)
        sin = np.concatenate([np.zeros((CTX_LEN, LANES), np.float32), sin], 0)
        return jnp.asarray(np.stack([cos, sin], 0).astype(np.float32))

    cr, sr, cc, sc = quarter(MLA_ROPE)
    ones = lambda n: np.ones((n_seq, n), np.float32)
    cos_a = np.concatenate([ones(MLA_NOPE), cr, cr, cc, cc, ones(32)], 1)
    sin_a = np.concatenate([0 * ones(MLA_NOPE), -sr, sr, -sc, sc, 0 * ones(32)], 1)
    cr, sr, cc, sc = quarter(GQA_HEAD_DIM)
    cos_b = np.concatenate([cr, cr, cc, cc] * 2, 1)
    sin_b = np.concatenate([-sr, sr, -sc, sc] * 2, 1)
    return with_ctx(cos_a, sin_a), with_ctx(cos_b, sin_b)


def _att_weights(w_in, w_uq, w_ukv):
    d = D_MODEL
    o = np.cumsum((0, MLA_Q_RANK, MLA_KV_RANK, MLA_ROPE, MLA_WIDTH, GQA_WIDTH, 128, 128, GQA_WIDTH))
    cq, ckv, kr, ga, qb, kb, vb, gb = (w_in[:, o[i]:o[i + 1]] for i in range(8))
    z = lambda n: jnp.zeros((d, n), F32)
    dup = lambda w: jnp.concatenate([w[:, 0:64], w[:, 0:64], w[:, 64:128], w[:, 64:128]], 1)
    win = jnp.concatenate([cq, ckv, z(MLA_NOPE), kr, z(32), ga, gb, qb, dup(kb), dup(vb)], 1)
    wq = w_uq.reshape(MLA_Q_RANK, MLA_HEADS, MLA_NOPE + MLA_ROPE)
    wq = jnp.pad(wq, ((0, 0), (0, 0), (0, 32))).reshape(MLA_Q_RANK, MLA_HEADS * LANES)
    wkv = w_ukv.reshape(MLA_KV_RANK, MLA_HEADS, MLA_NOPE + MLA_V)
    wk = jnp.pad(wkv[:, :, :MLA_NOPE], ((0, 0), (0, 0), (0, 64))).reshape(MLA_KV_RANK, MLA_HEADS * LANES)
    wv = wkv[:, :, MLA_NOPE:].reshape(MLA_KV_RANK, MLA_WIDTH)
    return win.astype(BF16), wq.astype(BF16), wk.astype(BF16), wv.astype(BF16)


def _attention_layer(xs, msel, w_in, w_out, q_norm_a, w_uq, kv_norm_a, w_ukv, q_norm_b, k_norm_b,
                     rope_a, rope_b, ln_g, ln_b, last):
    bsz, n_all, d = xs.shape
    nt = n_all // TM
    win, wq, wk, wv = _att_weights(w_in, w_uq, w_ukv)
    full = lambda shape: pl.BlockSpec(shape, lambda b, i: (0,) * len(shape))
    tok = lambda w: pl.BlockSpec((1, TM, w), lambda b, i: (b, i, 0))
    head = lambda n: pl.BlockSpec((1, n, TM, LANES), lambda b, i: (b, 0, i, 0))
    rope = pl.BlockSpec((2, TM, LANES), lambda b, i: (0, i, 0))
    sds = jax.ShapeDtypeStruct
    qa, ka, va, qb, kb, vb, sg = pl.pallas_call(
        _att_proj_kernel,
        grid=(bsz, nt),
        in_specs=[tok(d), _mod_spec(), full((d, ATT_COLS)), full((MLA_Q_RANK, MLA_HEADS * LANES)),
                  full((MLA_KV_RANK, MLA_HEADS * LANES)), full((MLA_KV_RANK, MLA_WIDTH)),
                  full((1, MLA_Q_RANK)), full((1, MLA_KV_RANK)), full((1, GQA_WIDTH)), full((1, LANES)),
                  rope, rope],
        out_specs=[head(MLA_HEADS), head(MLA_HEADS), tok(MLA_WIDTH), tok(GQA_WIDTH),
                   head(GQA_KV_HEADS), head(GQA_KV_HEADS), tok(2 * 512)],
        out_shape=[sds((bsz, MLA_HEADS, n_all, LANES), BF16), sds((bsz, MLA_HEADS, n_all, LANES), BF16),
                   sds((bsz, n_all, MLA_WIDTH), BF16), sds((bsz, n_all, GQA_WIDTH), BF16),
                   sds((bsz, GQA_KV_HEADS, n_all, LANES), BF16), sds((bsz, GQA_KV_HEADS, n_all, LANES), BF16),
                   sds((bsz, n_all, 2 * 512), BF16)],
        compiler_params=_params("parallel", "parallel"),
        name="att_proj",
    )(xs, msel, win, wq, wk, wv, q_norm_a[None, :], kv_norm_a[None, :],
      jnp.tile(q_norm_b, GQA_HEADS)[None, :], jnp.tile(k_norm_b, 2)[None, :], rope_a, rope_b)

    whole = lambda n: pl.BlockSpec((1, n, n_all, LANES), lambda b, i: (b, 0, 0, 0))
    mix_a = pl.pallas_call(
        functools.partial(_mla_attn_kernel, n_all=n_all),
        grid=(bsz, nt),
        in_specs=[head(MLA_HEADS), whole(MLA_HEADS), pl.BlockSpec((1, n_all, MLA_WIDTH), lambda b, i: (b, 0, 0)),
                  tok(MLA_WIDTH)],
        out_specs=tok(MLA_WIDTH),
        out_shape=sds((bsz, n_all, MLA_WIDTH), BF16),
        scratch_shapes=[pltpu.VMEM((TM, n_all), F32)] * 2,
        compiler_params=_params("parallel", "parallel"),
        name="mla_attention",
    )(qa, ka, va, sg)
    mix_b = pl.pallas_call(
        functools.partial(_gqa_attn_kernel, n_all=n_all),
        grid=(bsz, nt),
        in_specs=[tok(GQA_WIDTH), whole(GQA_KV_HEADS), whole(GQA_KV_HEADS),
                  pl.BlockSpec((1, TM, GQA_WIDTH), lambda b, i: (b, i, 1))],
        out_specs=tok(GQA_WIDTH),
        out_shape=sds((bsz, n_all, GQA_WIDTH), BF16),
        scratch_shapes=[pltpu.VMEM((TM, n_all), F32)] * 2,
        compiler_params=_params("parallel", "parallel"),
        name="gqa_attention",
    )(qb, kb, vb, sg)

    off = 1 if last else 0
    tok_in = lambda w: pl.BlockSpec((1, TM, w), lambda b, i: (b, i + off, 0))
    return pl.pallas_call(
        _att_out_kernel,
        grid=(bsz, nt - off),
        in_specs=[tok_in(MLA_WIDTH), tok_in(GQA_WIDTH), full((MLA_WIDTH + GQA_WIDTH, d)), tok_in(d),
                  _mod_spec(off), full((1, d)), full((1, d))],
        out_specs=tok(d),
        out_shape=sds((bsz, n_all - off * TM, d), F32),
        compiler_params=_params("parallel", "parallel"),
        name="att_out",
    )(mix_a, mix_b, w_out.astype(BF16), xs, msel, ln_g[None, :], ln_b[None, :])


def _rec_proj_kernel(x_ref, xp_ref, xn_ref, m_ref, win_ref, cw_ref, cb_ref, gp_ref,
                     q_ref, k_ref, v_ref, bg_ref, sg_ref, xl_ref, *, nt):
    i = pl.program_id(1)
    shift = m_ref[0, 0, 0:1, :]
    scale = m_ref[0, 0, 1:2, :]
    mod = lambda x: x * (1.0 + scale) + shift
    prev_ok = (i >= 2).astype(F32)
    next_ok = jnp.logical_and(i >= 1, i < nt - 1).astype(F32)
    h = mod(x_ref[0])
    hext = jnp.concatenate([mod(xp_ref[0]) * prev_ok, h, mod(xn_ref[0]) * next_ok], 0)
    pext = _bdot(hext, win_ref[:, 0:REC_CONV_COLS])
    rest = _bdot(h, win_ref[:, REC_CONV_COLS:])
    n_ext = TM + 2 * SUBLANES
    y = cw_ref[1:2, :] * pext[SUBLANES:SUBLANES + TM]
    for j in (0, 2, 3):
        y = y + cw_ref[j:j + 1, :] * pltpu.roll(pext, (1 - j) % n_ext, 0)[SUBLANES:SUBLANES + TM]
    qkv = _silu(y[:, 0:3 * GDN_WIDTH])
    xl_ref[0] = y[:, 3 * GDN_WIDTH:] + cb_ref[...]
    for hd in range(GDN_HEADS):
        sl = slice(hd * LANES, (hd + 1) * LANES)
        q = qkv[:, sl]
        k = qkv[:, GDN_WIDTH + hd * LANES:GDN_WIDTH + (hd + 1) * LANES]
        q_ref[0, :, sl] = q * (lax.rsqrt(jnp.sum(q * q, -1, keepdims=True) + EPS) * GDN_HEAD_DIM ** -0.5)
        k_ref[0, :, sl] = k * lax.rsqrt(jnp.sum(k * k, -1, keepdims=True) + EPS)
    v_ref[0] = qkv[:, 2 * GDN_WIDTH:]
    sg_ref[0] = _silu(rest[:, 0:2 * 512]).astype(BF16)
    ba = rest[:, 2 * 512:]
    t = ba + gp_ref[1:2, :]
    softplus = jnp.maximum(t, 0.0) + jnp.log(1.0 + jnp.exp(-jnp.abs(t)))
    decay = -jnp.exp(gp_ref[0:1, :]) * softplus
    bg_ref[0] = jnp.where(_lane_iota(ba.shape) < 2 * GDN_HEADS, jax.nn.sigmoid(ba), decay)


def _each(f, *lists):
    return [f(*args) for args in zip(*lists)]


def _dots(xs, ys):
    return _each(_bdot, xs, ys)


def _stacked_dots(tops, bots, ys):
    n = tops[0].shape[0]
    both = [_bdot(jnp.concatenate([t, b], 0), y) for t, b, y in zip(tops, bots, ys)]
    return [o[:n] for o in both], [o[n:] for o in both]


def _tri_inverse(ms, strict_blocks, eye):
    mds = [jnp.where(strict_blocks, 0.0, m) for m in ms]
    ls = [jnp.where(strict_blocks, m, 0.0) for m in ms]
    add = lambda xs, ys: _each(lambda x, y: x + y, xs, ys)
    ps = [eye - md for md in mds]
    pws = _dots(mds, mds)
    for _ in range(2):
        pws, pn = _stacked_dots(pws, ps, pws)
        ps = add(ps, pn)
    dinvs = add(ps, _dots(ps, pws))
    es = _dots(dinvs, ls)
    qs = [eye - e for e in es]
    pws = _dots(es, es)
    pws, qn = _stacked_dots(pws, qs, pws)
    qs = add(qs, qn)
    qs = add(qs, _dots(qs, pws))
    return _dots(qs, dinvs)


def _gdn_kernel(*refs):
    c = GDN_CHUNK
    ins, outs = (refs[0:4], refs[4:8]), refs[8:10]
    s_ref, u_ref, ops_ref, dec_ref = refs[10:14]

    @pl.when(pl.program_id(1) == 0)
    def _():
        for ref in (s_ref, u_ref, ops_ref, dec_ref):
            ref[...] = jnp.zeros_like(ref)

    streams = [(d, hd) for d in range(2) for hd in range(GDN_HEADS)]
    lanes = lambda hd: slice(hd * LANES, (hd + 1) * LANES)
    ss = [s_ref[d, hd] for d, hd in streams]
    on_state = _dots([ops_ref[d, hd, 0] for d, hd in streams], ss)
    v_news = _each(lambda u, ws: u - ws[:c], [u_ref[d, hd] for d, hd in streams], on_state)
    on_new = _dots([ops_ref[d, hd, 1] for d, hd in streams], v_news)
    for (d, hd), s, a, b in zip(streams, ss, on_state, on_new):
        s_ref[d, hd] = s * dec_ref[d, hd, 0:1, :] + b[c:]
        outs[d][0, :, lanes(hd)] = a[c:] + b[:c]

    row_i = lax.broadcasted_iota(jnp.int32, (c, c), 0)
    col_i = lax.broadcasted_iota(jnp.int32, (c, c), 1)
    eye = (row_i == col_i).astype(F32)
    strict_blocks = (row_i // GDN_DIAG) != (col_i // GDN_DIAG)
    incls = [row_i >= col_i, row_i <= col_i]
    stricts = [row_i > col_i, row_i < col_i]
    lasts = [c - 1, 0]
    bgs = [ins[d][3][0] for d in range(2)]

    qs = [ins[d][0][0, :, lanes(hd)] for d, hd in streams]
    ks = [ins[d][1][0, :, lanes(hd)] for d, hd in streams]
    vs = [ins[d][2][0, :, lanes(hd)] for d, hd in streams]
    betas = [bgs[d][:, d * GDN_HEADS + hd:d * GDN_HEADS + hd + 1] for d, hd in streams]
    gs = [jnp.broadcast_to(bgs[d][:, (2 + d) * GDN_HEADS + hd:(2 + d) * GDN_HEADS + hd + 1], (c, c))
          for d, hd in streams]
    gcums = []
    for d in range(2):
        parts = [_split_bf16(g, 3) for g, (sd, _) in zip(gs, streams) if sd == d]
        wide = jnp.dot(incls[d].astype(BF16), jnp.concatenate([p for ps in parts for p in ps], 1),
                       preferred_element_type=F32)
        gcums += [wide[:, (3 * i) * c:(3 * i + 1) * c] + wide[:, (3 * i + 1) * c:(3 * i + 2) * c]
                  + wide[:, (3 * i + 2) * c:(3 * i + 3) * c] for i in range(GDN_HEADS)]
    gcum_ts = [g.T for g in gcums]
    g_lasts = [g[lasts[d]:lasts[d] + 1, :] for g, (d, _) in zip(gcums, streams)]
    decays = [jnp.exp(jnp.where(incls[d], g - gt, -1e30)) for g, gt, (d, _) in zip(gcums, gcum_ts, streams)]
    es = [jnp.exp(g) for g in gcums]
    kts = [k.T for k in ks]
    kbs = _each(lambda k, b: k * b, ks, betas)
    kks, qks = _stacked_dots(kbs, qs, kts)
    ms = [jnp.where(stricts[d], kk * dec, 0.0) for kk, dec, (d, _) in zip(kks, decays, streams)]
    attns = _each(lambda a, dec: a * dec, qks, decays)
    ts = _tri_inverse(ms, strict_blocks, eye)
    sols = _each(lambda t, v, b, kb, e: _bdot(t, jnp.concatenate([v * b, kb * e], 1)), ts, vs, betas, kbs, es)
    for (d, hd), sol, attn, q, e, kt, gl, gt in zip(streams, sols, attns, qs, es, kts, g_lasts, gcum_ts):
        u_ref[d, hd] = sol[:, :LANES]
        ops_ref[d, hd, 0, 0:c] = sol[:, LANES:].astype(BF16)
        ops_ref[d, hd, 0, c:2 * c] = (q * e).astype(BF16)
        ops_ref[d, hd, 1, 0:c] = attn.astype(BF16)
        ops_ref[d, hd, 1, c:2 * c] = (kt * jnp.exp(gl - gt)).astype(BF16)
        dec_ref[d, hd] = jnp.broadcast_to(jnp.exp(gl), (SUBLANES, c))


def _lru_kernel(xl_ref, wg_ref, bgate_ref, lam_ref, o_ref, a_ref, b_ref, *, nt):
    w = xl_ref.shape[-1]
    rows = lax.broadcasted_iota(jnp.int32, (SUBLANES, w), 0)
    groups = TM // SUBLANES
    for d in range(2):
        rev = d == 1
        lam = lam_ref[d:d + 1, :]
        log_sig = jnp.minimum(lam, 0.0) - jnp.log(1.0 + jnp.exp(-jnp.abs(lam)))

        def tile_body(s, carry):
            if rev:
                t = jnp.where(s == 0, 0, nt - s)
            else:
                t = s
            base = pl.multiple_of(t * TM, TM)
            x = xl_ref[0, pl.ds(base, TM), :]
            xb = x.astype(BF16)
            sigmoid = lambda t: 0.5 * jnp.tanh(0.5 * t) + 0.5
            r = sigmoid(jnp.dot(xb, wg_ref[d, 0, 0], preferred_element_type=F32) + bgate_ref[d, 0, 0])
            gi = sigmoid(jnp.dot(xb, wg_ref[d, 1, 0], preferred_element_type=F32) + bgate_ref[d, 1, 0])
            a = jnp.exp(LRU_C * r * log_sig)
            a_ref[...] = a
            b_ref[...] = jnp.sqrt(1.0 - a * a) * (gi * x)

            def group_body(gidx, h_prev):
                gq = (groups - 1 - gidx) if rev else gidx
                off = pl.multiple_of(gq * SUBLANES, SUBLANES)
                a = a_ref[pl.ds(off, SUBLANES), :]
                bv = b_ref[pl.ds(off, SUBLANES), :]
                for sft in (1, 2, 4):
                    if rev:
                        ok = rows < SUBLANES - sft
                        a_s = pltpu.roll(a, SUBLANES - sft, 0)
                        b_s = pltpu.roll(bv, SUBLANES - sft, 0)
                    else:
                        ok = rows >= sft
                        a_s = pltpu.roll(a, sft, 0)
                        b_s = pltpu.roll(bv, sft, 0)
                    bv = jnp.where(ok, a * b_s + bv, bv)
                    a = jnp.where(ok, a * a_s, a)
                hcur = a * h_prev + bv
                dst = pl.ds(base + off, SUBLANES)
                if rev:
                    o_ref[0, dst, :] = o_ref[0, dst, :] + hcur
                    return jnp.broadcast_to(hcur[0:1, :], hcur.shape)
                o_ref[0, dst, :] = hcur
                return jnp.broadcast_to(hcur[SUBLANES - 1:SUBLANES, :], hcur.shape)

            return lax.fori_loop(0, groups, group_body, carry, unroll=4)

        lax.fori_loop(0, nt, tile_body, jnp.zeros((SUBLANES, w), F32))


def _rec_out_kernel(of_ref, ob_ref, r_ref, sg_ref, gn_ref, w_ref, x_ref, m_ref, g_ref, b_ref, o_ref):
    o = of_ref[0] + ob_ref[0]
    sg = sg_ref[0].astype(F32)
    y = jnp.dot((r_ref[0] * sg[:, GDN_WIDTH:]).astype(BF16), w_ref[GDN_WIDTH:, :], preferred_element_type=F32)
    for hd in range(GDN_HEADS):
        sl = slice(hd * LANES, (hd + 1) * LANES)
        oh = o[:, sl]
        og = oh * lax.rsqrt(jnp.mean(oh * oh, -1, keepdims=True) + EPS) * gn_ref[...] * sg[:, sl]
        y = y + jnp.dot(og.astype(BF16), w_ref[sl, :], preferred_element_type=F32)
    gate = m_ref[0, 0, 2:3, :]
    o_ref[0] = _layer_norm(DEEPNORM_ALPHA * x_ref[0] + gate * y, g_ref[...], b_ref[...])


def _rec_weights(w_in, gdn_conv_w, lru_conv_w, gdn_a_log, gdn_dt_bias, lru_gate_w):
    d = D_MODEL
    o = np.cumsum((0, 3 * GDN_WIDTH, GDN_WIDTH, 2 * GDN_HEADS, 2 * GDN_HEADS, LRU_WIDTH, LRU_WIDTH))
    qkv, z, b, a, xr, gr = (w_in[:, o[i]:o[i + 1]] for i in range(6))
    win = jnp.concatenate([qkv, xr, z, gr, b, a, jnp.zeros((d, LANES - 4 * GDN_HEADS), F32)], 1)
    conv_w = jnp.pad(jnp.concatenate([gdn_conv_w, lru_conv_w], 1), ((0, SUBLANES - 4), (0, 0)))
    pad = lambda p: jnp.pad(p.reshape(-1), (2 * GDN_HEADS, LANES - 4 * GDN_HEADS))
    gparams = jnp.pad(jnp.stack([pad(gdn_a_log), pad(gdn_dt_bias)], 0), ((0, SUBLANES - 2), (0, 0)))
    half = LRU_WIDTH // 2
    blocks = lru_gate_w.reshape(2, 2, 2, LRU_BLOCKS // 2, LRU_BLOCK_W, LRU_BLOCK_W)
    eye = jnp.eye(LRU_BLOCKS // 2, dtype=F32)
    wg = jnp.einsum('dghncm,nk->dghnckm', blocks, eye).reshape(2, 2, 2, half, half)
    return win.astype(BF16), conv_w, gparams, wg.astype(BF16)


def _recurrent_layer(xs, msel, w_in, w_out, gdn_conv_w, gdn_a_log, gdn_dt_bias, gdn_norm, lru_conv_w,
                     lru_conv_b, lru_gate_w, lru_gate_b, lru_lambda, ln_g, ln_b, last):
    bsz, n_all, d = xs.shape
    nt = n_all // TM
    win, conv_w, gparams, wg = _rec_weights(w_in, gdn_conv_w, lru_conv_w, gdn_a_log, gdn_dt_bias, lru_gate_w)
    full = lambda shape: pl.BlockSpec(shape, lambda b, i: (0,) * len(shape))
    tok = lambda w: pl.BlockSpec((1, TM, w), lambda b, i: (b, i, 0))
    sds = jax.ShapeDtypeStruct
    per = TM // SUBLANES
    n8 = n_all // SUBLANES
    q, k, v, bg, sg, xl = pl.pallas_call(
        functools.partial(_rec_proj_kernel, nt=nt),
        grid=(bsz, nt),
        in_specs=[tok(d),
                  pl.BlockSpec((1, SUBLANES, d), lambda b, i: (b, jnp.maximum(i * per - 1, 0), 0)),
                  pl.BlockSpec((1, SUBLANES, d), lambda b, i: (b, jnp.minimum((i + 1) * per, n8 - 1), 0)),
                  _mod_spec(), full((d, REC_COLS)), full((SUBLANES, REC_CONV_COLS)), full((1, LRU_WIDTH)),
                  full((SUBLANES, LANES))],
        out_specs=[tok(GDN_WIDTH), tok(GDN_WIDTH), tok(GDN_WIDTH), tok(LANES), tok(2 * 512), tok(LRU_WIDTH)],
        out_shape=[sds((bsz, n_all, GDN_WIDTH), F32)] * 3 + [sds((bsz, n_all, LANES), F32),
                   sds((bsz, n_all, 2 * 512), BF16), sds((bsz, n_all, LRU_WIDTH), F32)],
        compiler_params=_params("parallel", "parallel"),
        name="rec_proj",
    )(xs, xs, xs, msel, win, conv_w, lru_conv_b[None, :], gparams)

    nch = n_all // GDN_CHUNK
    nctx = CTX_LEN // GDN_CHUNK
    fwd = lambda s: s
    bwd = lambda s: jnp.where(s < nctx, nctx - 1 - s, nch - 1 + nctx - s)
    prep = lambda order: (lambda b, s: (b, order(jnp.minimum(s, nch - 1)), 0))
    apply_ = lambda order: (lambda b, s: (b, order(jnp.maximum(s - 1, 0)), 0))
    chunk_specs = lambda cmap: [pl.BlockSpec((1, GDN_CHUNK, GDN_WIDTH), cmap)] * 3 + [pl.BlockSpec((1, GDN_CHUNK, LANES), cmap)]
    per_stream = (2, GDN_HEADS)
    outs = pl.pallas_call(
        _gdn_kernel,
        grid=(bsz, nch + 1),
        in_specs=chunk_specs(prep(fwd)) + chunk_specs(prep(bwd)),
        out_specs=[pl.BlockSpec((1, GDN_CHUNK, GDN_WIDTH), apply_(fwd)),
                   pl.BlockSpec((1, GDN_CHUNK, GDN_WIDTH), apply_(bwd))],
        out_shape=[sds((bsz, n_all, GDN_WIDTH), F32)] * 2,
        scratch_shapes=[pltpu.VMEM(per_stream + (GDN_HEAD_DIM, GDN_HEAD_DIM), F32),
                        pltpu.VMEM(per_stream + (GDN_CHUNK, GDN_HEAD_DIM), F32),
                        pltpu.VMEM(per_stream + (2, 2 * GDN_CHUNK, GDN_CHUNK), BF16),
                        pltpu.VMEM(per_stream + (SUBLANES, GDN_CHUNK), F32)],
        compiler_params=_params("parallel", "arbitrary"),
        name="gdn",
    )(q, k, v, bg, q, k, v, bg)

    half = LRU_WIDTH // 2
    r = pl.pallas_call(
        functools.partial(_lru_kernel, nt=nt),
        grid=(bsz, 2),
        in_specs=[pl.BlockSpec((1, n_all, half), lambda b, c: (b, 0, c)),
                  pl.BlockSpec((2, 2, 1, half, half), lambda b, c: (0, 0, c, 0, 0)),
                  pl.BlockSpec((2, 2, 1, 1, half), lambda b, c: (0, 0, c, 0, 0)),
                  pl.BlockSpec((2, half), lambda b, c: (0, c))],
        out_specs=pl.BlockSpec((1, n_all, half), lambda b, c: (b, 0, c)),
        out_shape=sds((bsz, n_all, LRU_WIDTH), F32),
        scratch_shapes=[pltpu.VMEM((TM, half), F32), pltpu.VMEM((TM, half), F32)],
        compiler_params=_params("parallel", "parallel"),
        name="rg_lru",
    )(xl, wg, lru_gate_b.reshape(2, 2, 2, 1, half), lru_lambda)

    off = 1 if last else 0
    tok_in = lambda w: pl.BlockSpec((1, TM, w), lambda b, i: (b, i + off, 0))
    return pl.pallas_call(
        _rec_out_kernel,
        grid=(bsz, nt - off),
        in_specs=[tok_in(GDN_WIDTH), tok_in(GDN_WIDTH), tok_in(LRU_WIDTH), tok_in(2 * 512), full((1, LANES)),
                  full((GDN_WIDTH + LRU_WIDTH, d)), tok_in(d), _mod_spec(off), full((1, d)), full((1, d))],
        out_specs=tok(d),
        out_shape=sds((bsz, n_all - off * TM, d), F32),
        compiler_params=_params("parallel", "parallel"),
        name="rec_out",
    )(outs[0], outs[1], r, sg, gdn_norm[None, :], w_out.astype(BF16), xs, msel, ln_g[None, :], ln_b[None, :])


def kernel(x, c, ctx, c_ctx, mod_w, mod_b, ln_g, ln_b, att_w_in, att_w_out, mla_q_norm, mla_w_uq, mla_kv_norm, mla_w_ukv, gqa_q_norm, gqa_k_norm, rec_w_in, rec_w_out, gdn_conv_w, gdn_a_log, gdn_dt_bias, gdn_norm, lru_conv_w, lru_conv_b, lru_gate_w, lru_gate_b, lru_lambda):
    bsz, n_seq, _ = x.shape
    assert ctx.shape[1] == CTX_LEN == TM and n_seq % TM == 0 and bsz < SUBLANES
    rope_a, rope_b = _rope_tables(n_seq)
    mods = _modulation(c, c_ctx, mod_w, mod_b)
    xs = jnp.concatenate([ctx, x], 1)
    for layer in range(DEPTH):
        last = layer == DEPTH - 1
        li = layer // 2
        msel = _mod_select(mods[layer], bsz)
        if layer % 2 == 0:
            xs = _attention_layer(xs, msel, att_w_in[li], att_w_out[li], mla_q_norm[li], mla_w_uq[li],
                                  mla_kv_norm[li], mla_w_ukv[li], gqa_q_norm[li], gqa_k_norm[li],
                                  rope_a, rope_b, ln_g[layer], ln_b[layer], last)
        else:
            xs = _recurrent_layer(xs, msel, rec_w_in[li], rec_w_out[li], gdn_conv_w[li], gdn_a_log[li],
                                  gdn_dt_bias[li], gdn_norm[li], lru_conv_w[li], lru_conv_b[li],
                                  lru_gate_w[li], lru_gate_b[li], lru_lambda[li], ln_g[layer], ln_b[layer], last)
    return xs
```

```python
import functools

import numpy as np
import jax
import jax.numpy as jnp
from jax import lax
from jax.experimental import pallas as pl
from jax.experimental.pallas import tpu as pltpu

F32 = jnp.float32
BF16 = jnp.bfloat16

D_MODEL = 1024
DEPTH = 4
GRID_W = 64
CTX_LEN = 256
ROPE_THETA = 10000.0
EPS = 1e-6

MLA_HEADS = 8
MLA_Q_RANK = 256
MLA_KV_RANK = 128
MLA_NOPE = 64
MLA_ROPE = 32
MLA_V = 64
MLA_WIDTH = MLA_HEADS * MLA_V
GQA_HEADS = 8
GQA_KV_HEADS = 2
GQA_HEAD_DIM = 64
GQA_WIDTH = GQA_HEADS * GQA_HEAD_DIM
GDN_HEADS = 4
GDN_HEAD_DIM = 128
GDN_WIDTH = GDN_HEADS * GDN_HEAD_DIM
LRU_WIDTH = 512
LRU_BLOCKS = 8
LRU_BLOCK_W = LRU_WIDTH // LRU_BLOCKS
LRU_C = 8.0
DEEPNORM_ALPHA = (2 * DEPTH) ** 0.25

LANES = 128
SUBLANES = 8
TM = 256
GDN_CHUNK = 128
GDN_DIAG = 16
KEY_CHUNK = 512
PV_LAG = 2
LOG2_E = 1.4426950408889634
VMEM_LIMIT = 48 * 1024 * 1024

ATT_COLS = 2560
REC_CONV_COLS = 3 * GDN_WIDTH + LRU_WIDTH
REC_COLS = REC_CONV_COLS + 2 * 512 + LANES


def _params(*sem):
    return pltpu.CompilerParams(dimension_semantics=sem, vmem_limit_bytes=VMEM_LIMIT)


def _bdot(a, b):
    return jnp.dot(a.astype(BF16), b.astype(BF16), preferred_element_type=F32)


def _split_bf16(a, parts):
    out = []
    for _ in range(parts):
        hi = a.astype(BF16)
        out.append(hi)
        a = a - hi.astype(F32)
    return out


def _dot_split(a, b):
    a_hi, a_lo = _split_bf16(a, 2)
    b_hi, b_lo = _split_bf16(b, 2)
    d = functools.partial(jnp.dot, preferred_element_type=F32)
    return d(a_hi, b_hi) + (d(a_lo, b_hi) + d(a_hi, b_lo))


def _silu(x):
    return x * jax.nn.sigmoid(x)


def _lane_iota(shape):
    return lax.broadcasted_iota(jnp.int32, shape, len(shape) - 1)


def _swap_groups(x, n):
    fwd = pltpu.roll(x, LANES - n, 1)
    bwd = pltpu.roll(x, n, 1)
    return jnp.where((_lane_iota(x.shape) % (2 * n)) < n, fwd, bwd)


def _rope(x, cos, sin, n):
    return x * cos + _swap_groups(x, n) * sin


def _layer_norm(z, g, b):
    mu = jnp.mean(z, -1, keepdims=True)
    zc = z - mu
    var = jnp.mean(zc * zc, -1, keepdims=True)
    return zc * lax.rsqrt(var + EPS) * g + b


def _mod_kernel(c_ref, w_ref, b_ref, o_ref):
    c = c_ref[...]
    o_ref[0] = _dot_split(_silu(c), w_ref[0]) + b_ref[0]


def _modulation(c, c_ctx, mod_w, mod_b):
    bsz = c.shape[0]
    rows = jnp.concatenate([c, c_ctx[None, :], jnp.zeros((SUBLANES - bsz - 1, D_MODEL), F32)], 0)
    tn = 1024
    return pl.pallas_call(
        _mod_kernel,
        grid=(DEPTH, 3 * D_MODEL // tn),
        in_specs=[pl.BlockSpec((SUBLANES, D_MODEL), lambda l, n: (0, 0)),
                  pl.BlockSpec((1, D_MODEL, tn), lambda l, n: (l, 0, n)),
                  pl.BlockSpec((1, 1, tn), lambda l, n: (l, 0, n))],
        out_specs=pl.BlockSpec((1, SUBLANES, tn), lambda l, n: (l, 0, n)),
        out_shape=jax.ShapeDtypeStruct((DEPTH, SUBLANES, 3 * D_MODEL), F32),
        compiler_params=_params("parallel", "parallel"),
        name="modulation",
    )(rows, mod_w, mod_b.reshape(DEPTH, 1, 3 * D_MODEL))


def _mod_select(mods_l, bsz):
    m3 = mods_l.reshape(SUBLANES, 3, D_MODEL)
    ctx_m = jnp.broadcast_to(m3[bsz], (bsz, 3, D_MODEL))
    sel = jnp.stack([ctx_m, m3[:bsz]], 1)
    return jnp.pad(sel, ((0, 0), (0, 0), (0, SUBLANES - 3), (0, 0)))


def _mod_spec(off=0):
    return pl.BlockSpec((1, 1, SUBLANES, D_MODEL), lambda b, i: (b, jnp.minimum(i + off, 1), 0, 0))


def _att_proj_kernel(x_ref, m_ref, win_ref, wuq_ref, wk_ref, wv_ref, nq_ref, nkv_ref, nqb_ref, nkb_ref,
                     ra_ref, rb_ref, qa_ref, ka_ref, va_ref, qb_ref, kb_ref, vb_ref, sg_ref):
    shift = m_ref[0, 0, 0:1, :]
    scale = m_ref[0, 0, 1:2, :]
    h = x_ref[0] * (1.0 + scale) + shift
    p = _bdot(h, win_ref[...])
    cq, ckv, kr = p[:, 0:256], p[:, 256:384], p[:, 384:512]
    gates = p[:, 512:1536]
    qb, kb, vb = p[:, 1536:2048], p[:, 2048:2304], p[:, 2304:2560]
    sg_ref[0] = _silu(gates).astype(BF16)

    def rms(x):
        return x * lax.rsqrt(jnp.mean(x * x, -1, keepdims=True) + EPS)

    qa = _bdot(rms(cq) * nq_ref[...], wuq_ref[...])
    ckvn = rms(ckv) * nkv_ref[...]
    kn = _bdot(ckvn, wk_ref[...])
    va = _bdot(ckvn, wv_ref[...])
    for j in range(MLA_HEADS // 2):
        va_ref[0, j] = va[:, j * LANES:(j + 1) * LANES].T.astype(BF16)
    cos_a, sin_a = ra_ref[0], ra_ref[1]
    kr = _rope(kr, cos_a, sin_a, MLA_ROPE // 4)
    q_scale = LOG2_E * (MLA_NOPE + MLA_ROPE) ** -0.5
    for hd in range(MLA_HEADS):
        sl = slice(hd * LANES, (hd + 1) * LANES)
        qa_ref[0, hd] = (_rope(qa[:, sl], cos_a, sin_a, MLA_ROPE // 4) * q_scale).T.astype(BF16)
        ka_ref[0, hd] = (kn[:, sl] + kr).astype(BF16)

    cos_b, sin_b = rb_ref[0], rb_ref[1]
    low = _lane_iota((TM, LANES)) < GQA_HEAD_DIM
    for c in range(GQA_HEADS // 2):
        sl = slice(c * LANES, (c + 1) * LANES)
        xq = qb[:, sl]
        sq = xq * xq
        s_lo = jnp.sum(jnp.where(low, sq, 0.0), -1, keepdims=True)
        s_hi = jnp.sum(jnp.where(low, 0.0, sq), -1, keepdims=True)
        ms = jnp.where(low, s_lo, s_hi) * (1.0 / GQA_HEAD_DIM)
        xn = xq * lax.rsqrt(ms + EPS) * nqb_ref[:, sl]
        qt = (_rope(xn, cos_b, sin_b, GQA_HEAD_DIM // 4) * (LOG2_E * GQA_HEAD_DIM ** -0.5)).T
        top = lax.broadcasted_iota(jnp.int32, qt.shape, 0) < GQA_HEAD_DIM
        qb_ref[0, 2 * c] = jnp.where(top, qt, 0.0).astype(BF16)
        qb_ref[0, 2 * c + 1] = jnp.where(top, 0.0, qt).astype(BF16)
    for g in range(GQA_KV_HEADS):
        sl = slice(g * LANES, (g + 1) * LANES)
        kb_ref[0, g] = _rope(rms(kb[:, sl]) * nkb_ref[...], cos_b, sin_b, GQA_HEAD_DIM // 4).astype(BF16)
        vb_ref[0, g] = vb[:, sl].T.astype(BF16)


def _fold_rows(op, x, acc):
    for r in range(x.shape[0] // SUBLANES):
        acc = op(acc, x[r * SUBLANES:(r + 1) * SUBLANES])
    return acc


def _attend_heads(qtf, kf, vtf, s_refs, n_heads, nk):
    n_chunks = max(nk // KEY_CHUNK, 1)
    bounds = [i * KEY_CHUNK for i in range(n_chunks)] + [nk]
    chunks = [(bounds[i], bounds[i + 1] - bounds[i]) for i in range(n_chunks)]

    def score(h, c0, n, mt):
        s = jnp.dot(kf(h, c0, n), qtf(h), preferred_element_type=F32)
        s_refs[h % 2][c0:c0 + n, :] = s
        return _fold_rows(jnp.maximum, s, mt)

    neg = jnp.full((SUBLANES, TM), -jnp.inf, F32)
    zero = jnp.zeros((SUBLANES, TM), F32)
    acc = [None] * n_heads
    col_sum = [zero] * n_heads
    pending = []

    def pv_oldest():
        h, p, c0, n = pending.pop(0)
        o = jnp.dot(vtf(h, c0, n), p, preferred_element_type=F32)
        acc[h] = o if acc[h] is None else acc[h] + o

    mt = neg
    for c0, n in chunks:
        mt = score(0, c0, n, mt)
    for h in range(n_heads):
        m = jnp.max(mt, 0, keepdims=True)
        mt = neg
        for c0, n in chunks:
            p = jnp.exp2(s_refs[h % 2][c0:c0 + n, :] - m)
            col_sum[h] = _fold_rows(jnp.add, p, col_sum[h])
            pending.append((h, p.astype(BF16), c0, n))
            if h + 1 < n_heads:
                mt = score(h + 1, c0, n, mt)
            if len(pending) > PV_LAG:
                pv_oldest()
    while pending:
        pv_oldest()
    return [acc[h] / jnp.sum(col_sum[h], 0, keepdims=True) for h in range(n_heads)]


def _store_head_pairs(outs, sg_ref, o_ref):
    for j in range(len(outs) // 2):
        sl = slice(j * LANES, (j + 1) * LANES)
        pair = jnp.concatenate([outs[2 * j], outs[2 * j + 1]], 0).T
        o_ref[0, :, sl] = (pair * sg_ref[0, :, sl].astype(F32)).astype(BF16)


def _attn_kernel(qt_ref, k_ref, vt_ref, sg_ref, o_ref, sa_ref, sb_ref, *, n_all, n_heads, dv, k_of, v_of):
    def vt_chunk(h, c0, n):
        blk, r0 = v_of(h)
        return vt_ref[0, blk, r0:r0 + dv, c0:c0 + n]

    def run(nk):
        outs = _attend_heads(lambda h: qt_ref[0, h], lambda h, c0, n: k_ref[0, k_of(h), c0:c0 + n, :],
                             vt_chunk, (sa_ref, sb_ref), n_heads, nk)
        _store_head_pairs(outs, sg_ref, o_ref)

    @pl.when(pl.program_id(1) == 0)
    def _():
        run(CTX_LEN)

    @pl.when(pl.program_id(1) > 0)
    def _():
        run(n_all)


def _att_out_kernel(ma_ref, mb_ref, w_ref, x_ref, m_ref, g_ref, b_ref, o_ref):
    y = (jnp.dot(ma_ref[0], w_ref[0:MLA_WIDTH, :], preferred_element_type=F32)
         + jnp.dot(mb_ref[0], w_ref[MLA_WIDTH:, :], preferred_element_type=F32))
    gate = m_ref[0, 0, 2:3, :]
    o_ref[0] = _layer_norm(DEEPNORM_ALPHA * x_ref[0] + gate * y, g_ref[...], b_ref[...])


def _rope_tables(n_seq):
    pos = np.arange(n_seq)
    row = (pos // GRID_W).astype(np.float32)[:, None]
    col = (pos % GRID_W).astype(np.float32)[:, None]

    def quarter(dim):
        half = dim // 2
        inv = (ROPE_THETA ** (-np.arange(0, half, 2, dtype=np.float32) / half)).astype(np.float32)
        return np.cos(row * inv), np.sin(row * inv), np.cos(col * inv), np.sin(col * inv)

    def with_ctx(cos, sin):
        cos = np.concatenate([np.ones((CTX_LEN, LANES), np.float32), cos], 0)
        sin = np.concatenate([np.zeros((CTX_LEN, LANES), np.float32), sin], 0)
        return jnp.asarray(np.stack([cos, sin], 0).astype(np.float32))

    cr, sr, cc, sc = quarter(MLA_ROPE)
    ones = lambda n: np.ones((n_seq, n), np.float32)
    cos_a = np.concatenate([ones(MLA_NOPE), cr, cr, cc, cc, ones(32)], 1)
    sin_a = np.concatenate([0 * ones(MLA_NOPE), -sr, sr, -sc, sc, 0 * ones(32)], 1)
    cr, sr, cc, sc = quarter(GQA_HEAD_DIM)
    cos_b = np.concatenate([cr, cr, cc, cc] * 2, 1)
    sin_b = np.concatenate([-sr, sr, -sc, sc] * 2, 1)
    return with_ctx(cos_a, sin_a), with_ctx(cos_b, sin_b)


def _att_weights(w_in, w_uq, w_ukv):
    d = D_MODEL
    o = np.cumsum((0, MLA_Q_RANK, MLA_KV_RANK, MLA_ROPE, MLA_WIDTH, GQA_WIDTH, 128, 128, GQA_WIDTH))
    cq, ckv, kr, ga, qb, kb, vb, gb = (w_in[:, o[i]:o[i + 1]] for i in range(8))
    z = lambda n: jnp.zeros((d, n), F32)
    dup = lambda w: jnp.concatenate([w[:, 0:64], w[:, 0:64], w[:, 64:128], w[:, 64:128]], 1)
    win = jnp.concatenate([cq, ckv, z(MLA_NOPE), kr, z(32), ga, gb, qb, dup(kb), dup(vb)], 1)
    wq = w_uq.reshape(MLA_Q_RANK, MLA_HEADS, MLA_NOPE + MLA_ROPE)
    wq = jnp.pad(wq, ((0, 0), (0, 0), (0, 32))).reshape(MLA_Q_RANK, MLA_HEADS * LANES)
    wkv = w_ukv.reshape(MLA_KV_RANK, MLA_HEADS, MLA_NOPE + MLA_V)
    wk = jnp.pad(wkv[:, :, :MLA_NOPE], ((0, 0), (0, 0), (0, 64))).reshape(MLA_KV_RANK, MLA_HEADS * LANES)
    wv = wkv[:, :, MLA_NOPE:].reshape(MLA_KV_RANK, MLA_WIDTH)
    return win.astype(BF16), wq.astype(BF16), wk.astype(BF16), wv.astype(BF16)


def _attention_layer(xs, msel, w_in, w_out, q_norm_a, w_uq, kv_norm_a, w_ukv, q_norm_b, k_norm_b,
                     rope_a, rope_b, ln_g, ln_b, last):
    bsz, n_all, d = xs.shape
    nt = n_all // TM
    win, wq, wk, wv = _att_weights(w_in, w_uq, w_ukv)
    full = lambda shape: pl.BlockSpec(shape, lambda b, i: (0,) * len(shape))
    tok = lambda w: pl.BlockSpec((1, TM, w), lambda b, i: (b, i, 0))
    head = lambda n: pl.BlockSpec((1, n, TM, LANES), lambda b, i: (b, 0, i, 0))
    head_t = lambda n: pl.BlockSpec((1, n, LANES, TM), lambda b, i: (b, 0, 0, i))
    rope = pl.BlockSpec((2, TM, LANES), lambda b, i: (0, i, 0))
    sds = jax.ShapeDtypeStruct
    qa, ka, va, qb, kb, vb, sg = pl.pallas_call(
        _att_proj_kernel,
        grid=(bsz, nt),
        in_specs=[tok(d), _mod_spec(), full((d, ATT_COLS)), full((MLA_Q_RANK, MLA_HEADS * LANES)),
                  full((MLA_KV_RANK, MLA_HEADS * LANES)), full((MLA_KV_RANK, MLA_WIDTH)),
                  full((1, MLA_Q_RANK)), full((1, MLA_KV_RANK)), full((1, GQA_WIDTH)), full((1, LANES)),
                  rope, rope],
        out_specs=[head_t(MLA_HEADS), head(MLA_HEADS), head_t(MLA_HEADS // 2), head_t(GQA_HEADS),
                   head(GQA_KV_HEADS), head_t(GQA_KV_HEADS), tok(2 * 512)],
        out_shape=[sds((bsz, MLA_HEADS, LANES, n_all), BF16), sds((bsz, MLA_HEADS, n_all, LANES), BF16),
                   sds((bsz, MLA_HEADS // 2, LANES, n_all), BF16), sds((bsz, GQA_HEADS, LANES, n_all), BF16),
                   sds((bsz, GQA_KV_HEADS, n_all, LANES), BF16), sds((bsz, GQA_KV_HEADS, LANES, n_all), BF16),
                   sds((bsz, n_all, 2 * 512), BF16)],
        compiler_params=_params("parallel", "parallel"),
        name="att_proj",
    )(xs, msel, win, wq, wk, wv, q_norm_a[None, :], kv_norm_a[None, :],
      jnp.tile(q_norm_b, GQA_HEADS)[None, :], jnp.tile(k_norm_b, 2)[None, :], rope_a, rope_b)

    whole = lambda n: pl.BlockSpec((1, n, n_all, LANES), lambda b, i: (b, 0, 0, 0))
    whole_t = lambda n: pl.BlockSpec((1, n, LANES, n_all), lambda b, i: (b, 0, 0, 0))
    mix_a = pl.pallas_call(
        functools.partial(_attn_kernel, n_all=n_all, n_heads=MLA_HEADS, dv=MLA_V, k_of=lambda h: h,
                          v_of=lambda h: (h // 2, (h % 2) * MLA_V)),
        grid=(bsz, nt),
        in_specs=[head_t(MLA_HEADS), whole(MLA_HEADS), whole_t(MLA_HEADS // 2), tok(MLA_WIDTH)],
        out_specs=tok(MLA_WIDTH),
        out_shape=sds((bsz, n_all, MLA_WIDTH), BF16),
        scratch_shapes=[pltpu.VMEM((n_all, TM), F32)] * 2,
        compiler_params=_params("parallel", "parallel"),
        name="mla_attention",
    )(qa, ka, va, sg)
    group = GQA_HEADS // GQA_KV_HEADS
    mix_b = pl.pallas_call(
        functools.partial(_attn_kernel, n_all=n_all, n_heads=GQA_HEADS, dv=GQA_HEAD_DIM, k_of=lambda h: h // group,
                          v_of=lambda h: (h // group, 0)),
        grid=(bsz, nt),
        in_specs=[head_t(GQA_HEADS), whole(GQA_KV_HEADS), whole_t(GQA_KV_HEADS),
                  pl.BlockSpec((1, TM, GQA_WIDTH), lambda b, i: (b, i, 1))],
        out_specs=tok(GQA_WIDTH),
        out_shape=sds((bsz, n_all, GQA_WIDTH), BF16),
        scratch_shapes=[pltpu.VMEM((n_all, TM), F32)] * 2,
        compiler_params=_params("parallel", "parallel"),
        name="gqa_attention",
    )(qb, kb, vb, sg)

    off = 1 if last else 0
    tok_in = lambda w: pl.BlockSpec((1, TM, w), lambda b, i: (b, i + off, 0))
    return pl.pallas_call(
        _att_out_kernel,
        grid=(bsz, nt - off),
        in_specs=[tok_in(MLA_WIDTH), tok_in(GQA_WIDTH), full((MLA_WIDTH + GQA_WIDTH, d)), tok_in(d),
                  _mod_spec(off), full((1, d)), full((1, d))],
        out_specs=tok(d),
        out_shape=sds((bsz, n_all - off * TM, d), F32),
        compiler_params=_params("parallel", "parallel"),
        name="att_out",
    )(mix_a, mix_b, w_out.astype(BF16), xs, msel, ln_g[None, :], ln_b[None, :])


def _rec_proj_kernel(x_ref, xp_ref, xn_ref, m_ref, win_ref, cw_ref, cb_ref, gp_ref,
                     q_ref, k_ref, v_ref, bg_ref, sg_ref, xl_ref, *, nt):
    i = pl.program_id(1)
    shift = m_ref[0, 0, 0:1, :]
    scale = m_ref[0, 0, 1:2, :]
    mod = lambda x: x * (1.0 + scale) + shift
    prev_ok = (i >= 2).astype(F32)
    next_ok = jnp.logical_and(i >= 1, i < nt - 1).astype(F32)
    h = mod(x_ref[0])
    hext = jnp.concatenate([mod(xp_ref[0]) * prev_ok, h, mod(xn_ref[0]) * next_ok], 0)
    pext = _bdot(hext, win_ref[:, 0:REC_CONV_COLS])
    rest = _bdot(h, win_ref[:, REC_CONV_COLS:])
    n_ext = TM + 2 * SUBLANES
    y = cw_ref[1:2, :] * pext[SUBLANES:SUBLANES + TM]
    for j in (0, 2, 3):
        y = y + cw_ref[j:j + 1, :] * pltpu.roll(pext, (1 - j) % n_ext, 0)[SUBLANES:SUBLANES + TM]
    qkv = _silu(y[:, 0:3 * GDN_WIDTH])
    xl_ref[0] = y[:, 3 * GDN_WIDTH:] + cb_ref[...]
    for hd in range(GDN_HEADS):
        sl = slice(hd * LANES, (hd + 1) * LANES)
        q = qkv[:, sl]
        k = qkv[:, GDN_WIDTH + hd * LANES:GDN_WIDTH + (hd + 1) * LANES]
        q_ref[0, :, sl] = q * (lax.rsqrt(jnp.sum(q * q, -1, keepdims=True) + EPS) * GDN_HEAD_DIM ** -0.5)
        k_ref[0, :, sl] = k * lax.rsqrt(jnp.sum(k * k, -1, keepdims=True) + EPS)
    v_ref[0] = qkv[:, 2 * GDN_WIDTH:]
    sg_ref[0] = _silu(rest[:, 0:2 * 512]).astype(BF16)
    ba = rest[:, 2 * 512:]
    t = ba + gp_ref[1:2, :]
    softplus = jnp.maximum(t, 0.0) + jnp.log(1.0 + jnp.exp(-jnp.abs(t)))
    decay = -jnp.exp(gp_ref[0:1, :]) * softplus
    bg_ref[0] = jnp.where(_lane_iota(ba.shape) < 2 * GDN_HEADS, jax.nn.sigmoid(ba), decay)


def _each(f, *lists):
    return [f(*args) for args in zip(*lists)]


def _dots(xs, ys):
    return _each(_bdot, xs, ys)


def _stacked_dots(tops, bots, ys):
    n = tops[0].shape[0]
    both = [_bdot(jnp.concatenate([t, b], 0), y) for t, b, y in zip(tops, bots, ys)]
    return [o[:n] for o in both], [o[n:] for o in both]


def _tri_inverse(ms, strict_blocks, eye):
    mds = [jnp.where(strict_blocks, 0.0, m) for m in ms]
    ls = [jnp.where(strict_blocks, m, 0.0) for m in ms]
    add = lambda xs, ys: _each(lambda x, y: x + y, xs, ys)
    ps = [eye - md for md in mds]
    pws = _dots(mds, mds)
    for _ in range(2):
        pws, pn = _stacked_dots(pws, ps, pws)
        ps = add(ps, pn)
    dinvs = add(ps, _dots(ps, pws))
    es = _dots(dinvs, ls)
    qs = [eye - e for e in es]
    pws = _dots(es, es)
    pws, qn = _stacked_dots(pws, qs, pws)
    qs = add(qs, qn)
    qs = add(qs, _dots(qs, pws))
    return _dots(qs, dinvs)


def _gdn_kernel(*refs):
    c = GDN_CHUNK
    ins, outs = (refs[0:4], refs[4:8]), refs[8:10]
    s_ref, u_ref, ops_ref, dec_ref = refs[10:14]

    @pl.when(pl.program_id(1) == 0)
    def _():
        for ref in (s_ref, u_ref, ops_ref, dec_ref):
            ref[...] = jnp.zeros_like(ref)

    streams = [(d, hd) for d in range(2) for hd in range(GDN_HEADS)]
    lanes = lambda hd: slice(hd * LANES, (hd + 1) * LANES)
    ss = [s_ref[d, hd] for d, hd in streams]
    on_state = _dots([ops_ref[d, hd, 0] for d, hd in streams], ss)
    v_news = _each(lambda u, ws: u - ws[:c], [u_ref[d, hd] for d, hd in streams], on_state)
    on_new = _dots([ops_ref[d, hd, 1] for d, hd in streams], v_news)
    for (d, hd), s, a, b in zip(streams, ss, on_state, on_new):
        s_ref[d, hd] = s * dec_ref[d, hd, 0:1, :] + b[c:]
        outs[d][0, :, lanes(hd)] = a[c:] + b[:c]

    row_i = lax.broadcasted_iota(jnp.int32, (c, c), 0)
    col_i = lax.broadcasted_iota(jnp.int32, (c, c), 1)
    eye = (row_i == col_i).astype(F32)
    strict_blocks = (row_i // GDN_DIAG) != (col_i // GDN_DIAG)
    incls = [row_i >= col_i, row_i <= col_i]
    stricts = [row_i > col_i, row_i < col_i]
    lasts = [c - 1, 0]
    bgs = [ins[d][3][0] for d in range(2)]

    qs = [ins[d][0][0, :, lanes(hd)] for d, hd in streams]
    ks = [ins[d][1][0, :, lanes(hd)] for d, hd in streams]
    vs = [ins[d][2][0, :, lanes(hd)] for d, hd in streams]
    betas = [bgs[d][:, d * GDN_HEADS + hd:d * GDN_HEADS + hd + 1] for d, hd in streams]
    gs = [jnp.broadcast_to(bgs[d][:, (2 + d) * GDN_HEADS + hd:(2 + d) * GDN_HEADS + hd + 1], (c, c))
          for d, hd in streams]
    gcums = []
    for d in range(2):
        parts = [_split_bf16(g, 3) for g, (sd, _) in zip(gs, streams) if sd == d]
        wide = jnp.dot(incls[d].astype(BF16), jnp.concatenate([p for ps in parts for p in ps], 1),
                       preferred_element_type=F32)
        gcums += [wide[:, (3 * i) * c:(3 * i + 1) * c] + wide[:, (3 * i + 1) * c:(3 * i + 2) * c]
                  + wide[:, (3 * i + 2) * c:(3 * i + 3) * c] for i in range(GDN_HEADS)]
    gcum_ts = [g.T for g in gcums]
    g_lasts = [g[lasts[d]:lasts[d] + 1, :] for g, (d, _) in zip(gcums, streams)]
    decays = [jnp.exp(jnp.where(incls[d], g - gt, -1e30)) for g, gt, (d, _) in zip(gcums, gcum_ts, streams)]
    es = [jnp.exp(g) for g in gcums]
    kts = [k.T for k in ks]
    kbs = _each(lambda k, b: k * b, ks, betas)
    kks, qks = _stacked_dots(kbs, qs, kts)
    ms = [jnp.where(stricts[d], kk * dec, 0.0) for kk, dec, (d, _) in zip(kks, decays, streams)]
    attns = _each(lambda a, dec: a * dec, qks, decays)
    ts = _tri_inverse(ms, strict_blocks, eye)
    sols = _each(lambda t, v, b, kb, e: _bdot(t, jnp.concatenate([v * b, kb * e], 1)), ts, vs, betas, kbs, es)
    for (d, hd), sol, attn, q, e, kt, gl, gt in zip(streams, sols, attns, qs, es, kts, g_lasts, gcum_ts):
        u_ref[d, hd] = sol[:, :LANES]
        ops_ref[d, hd, 0, 0:c] = sol[:, LANES:].astype(BF16)
        ops_ref[d, hd, 0, c:2 * c] = (q * e).astype(BF16)
        ops_ref[d, hd, 1, 0:c] = attn.astype(BF16)
        ops_ref[d, hd, 1, c:2 * c] = (kt * jnp.exp(gl - gt)).astype(BF16)
        dec_ref[d, hd] = jnp.broadcast_to(jnp.exp(gl), (SUBLANES, c))


def _lru_kernel(xl_ref, wg_ref, bgate_ref, lam_ref, o_ref, a_ref, b_ref, *, nt):
    w = xl_ref.shape[-1]
    rows = lax.broadcasted_iota(jnp.int32, (SUBLANES, w), 0)
    groups = TM // SUBLANES
    for d in range(2):
        rev = d == 1
        lam = lam_ref[d:d + 1, :]
        log_sig = jnp.minimum(lam, 0.0) - jnp.log(1.0 + jnp.exp(-jnp.abs(lam)))

        def tile_body(s, carry):
            if rev:
                t = jnp.where(s == 0, 0, nt - s)
            else:
                t = s
            base = pl.multiple_of(t * TM, TM)
            x = xl_ref[0, pl.ds(base, TM), :]
            xb = x.astype(BF16)
            sigmoid = lambda t: 0.5 * jnp.tanh(0.5 * t) + 0.5
            r = sigmoid(jnp.dot(xb, wg_ref[d, 0, 0], preferred_element_type=F32) + bgate_ref[d, 0, 0])
            gi = sigmoid(jnp.dot(xb, wg_ref[d, 1, 0], preferred_element_type=F32) + bgate_ref[d, 1, 0])
            a = jnp.exp(LRU_C * r * log_sig)
            a_ref[...] = a
            b_ref[...] = jnp.sqrt(1.0 - a * a) * (gi * x)

            def group_body(gidx, h_prev):
                gq = (groups - 1 - gidx) if rev else gidx
                off = pl.multiple_of(gq * SUBLANES, SUBLANES)
                a = a_ref[pl.ds(off, SUBLANES), :]
                bv = b_ref[pl.ds(off, SUBLANES), :]
                for sft in (1, 2, 4):
                    if rev:
                        ok = rows < SUBLANES - sft
                        a_s = pltpu.roll(a, SUBLANES - sft, 0)
                        b_s = pltpu.roll(bv, SUBLANES - sft, 0)
                    else:
                        ok = rows >= sft
                        a_s = pltpu.roll(a, sft, 0)
                        b_s = pltpu.roll(bv, sft, 0)
                    bv = jnp.where(ok, a * b_s + bv, bv)
                    a = jnp.where(ok, a * a_s, a)
                hcur = a * h_prev + bv
                dst = pl.ds(base + off, SUBLANES)
                if rev:
                    o_ref[0, dst, :] = o_ref[0, dst, :] + hcur
                    return jnp.broadcast_to(hcur[0:1, :], hcur.shape)
                o_ref[0, dst, :] = hcur
                return jnp.broadcast_to(hcur[SUBLANES - 1:SUBLANES, :], hcur.shape)

            return lax.fori_loop(0, groups, group_body, carry, unroll=4)

        lax.fori_loop(0, nt, tile_body, jnp.zeros((SUBLANES, w), F32))


def _rec_out_kernel(of_ref, ob_ref, r_ref, sg_ref, gn_ref, w_ref, x_ref, m_ref, g_ref, b_ref, o_ref):
    o = of_ref[0] + ob_ref[0]
    sg = sg_ref[0].astype(F32)
    y = jnp.dot((r_ref[0] * sg[:, GDN_WIDTH:]).astype(BF16), w_ref[GDN_WIDTH:, :], preferred_element_type=F32)
    for hd in range(GDN_HEADS):
        sl = slice(hd * LANES, (hd + 1) * LANES)
        oh = o[:, sl]
        og = oh * lax.rsqrt(jnp.mean(oh * oh, -1, keepdims=True) + EPS) * gn_ref[...] * sg[:, sl]
        y = y + jnp.dot(og.astype(BF16), w_ref[sl, :], preferred_element_type=F32)
    gate = m_ref[0, 0, 2:3, :]
    o_ref[0] = _layer_norm(DEEPNORM_ALPHA * x_ref[0] + gate * y, g_ref[...], b_ref[...])


def _rec_weights(w_in, gdn_conv_w, lru_conv_w, gdn_a_log, gdn_dt_bias, lru_gate_w):
    d = D_MODEL
    o = np.cumsum((0, 3 * GDN_WIDTH, GDN_WIDTH, 2 * GDN_HEADS, 2 * GDN_HEADS, LRU_WIDTH, LRU_WIDTH))
    qkv, z, b, a, xr, gr = (w_in[:, o[i]:o[i + 1]] for i in range(6))
    win = jnp.concatenate([qkv, xr, z, gr, b, a, jnp.zeros((d, LANES - 4 * GDN_HEADS), F32)], 1)
    conv_w = jnp.pad(jnp.concatenate([gdn_conv_w, lru_conv_w], 1), ((0, SUBLANES - 4), (0, 0)))
    pad = lambda p: jnp.pad(p.reshape(-1), (2 * GDN_HEADS, LANES - 4 * GDN_HEADS))
    gparams = jnp.pad(jnp.stack([pad(gdn_a_log), pad(gdn_dt_bias)], 0), ((0, SUBLANES - 2), (0, 0)))
    half = LRU_WIDTH // 2
    blocks = lru_gate_w.reshape(2, 2, 2, LRU_BLOCKS // 2, LRU_BLOCK_W, LRU_BLOCK_W)
    eye = jnp.eye(LRU_BLOCKS // 2, dtype=F32)
    wg = jnp.einsum('dghncm,nk->dghnckm', blocks, eye).reshape(2, 2, 2, half, half)
    return win.astype(BF16), conv_w, gparams, wg.astype(BF16)


def _recurrent_layer(xs, msel, w_in, w_out, gdn_conv_w, gdn_a_log, gdn_dt_bias, gdn_norm, lru_conv_w,
                     lru_conv_b, lru_gate_w, lru_gate_b, lru_lambda, ln_g, ln_b, last):
    bsz, n_all, d = xs.shape
    nt = n_all // TM
    win, conv_w, gparams, wg = _rec_weights(w_in, gdn_conv_w, lru_conv_w, gdn_a_log, gdn_dt_bias, lru_gate_w)
    full = lambda shape: pl.BlockSpec(shape, lambda b, i: (0,) * len(shape))
    tok = lambda w: pl.BlockSpec((1, TM, w), lambda b, i: (b, i, 0))
    sds = jax.ShapeDtypeStruct
    per = TM // SUBLANES
    n8 = n_all // SUBLANES
    q, k, v, bg, sg, xl = pl.pallas_call(
        functools.partial(_rec_proj_kernel, nt=nt),
        grid=(bsz, nt),
        in_specs=[tok(d),
                  pl.BlockSpec((1, SUBLANES, d), lambda b, i: (b, jnp.maximum(i * per - 1, 0), 0)),
                  pl.BlockSpec((1, SUBLANES, d), lambda b, i: (b, jnp.minimum((i + 1) * per, n8 - 1), 0)),
                  _mod_spec(), full((d, REC_COLS)), full((SUBLANES, REC_CONV_COLS)), full((1, LRU_WIDTH)),
                  full((SUBLANES, LANES))],
        out_specs=[tok(GDN_WIDTH), tok(GDN_WIDTH), tok(GDN_WIDTH), tok(LANES), tok(2 * 512), tok(LRU_WIDTH)],
        out_shape=[sds((bsz, n_all, GDN_WIDTH), F32)] * 3 + [sds((bsz, n_all, LANES), F32),
                   sds((bsz, n_all, 2 * 512), BF16), sds((bsz, n_all, LRU_WIDTH), F32)],
        compiler_params=_params("parallel", "parallel"),
        name="rec_proj",
    )(xs, xs, xs, msel, win, conv_w, lru_conv_b[None, :], gparams)

    nch = n_all // GDN_CHUNK
    nctx = CTX_LEN // GDN_CHUNK
    fwd = lambda s: s
    bwd = lambda s: jnp.where(s < nctx, nctx - 1 - s, nch - 1 + nctx - s)
    prep = lambda order: (lambda b, s: (b, order(jnp.minimum(s, nch - 1)), 0))
    apply_ = lambda order: (lambda b, s: (b, order(jnp.maximum(s - 1, 0)), 0))
    chunk_specs = lambda cmap: [pl.BlockSpec((1, GDN_CHUNK, GDN_WIDTH), cmap)] * 3 + [pl.BlockSpec((1, GDN_CHUNK, LANES), cmap)]
    per_stream = (2, GDN_HEADS)
    outs = pl.pallas_call(
        _gdn_kernel,
        grid=(bsz, nch + 1),
        in_specs=chunk_specs(prep(fwd)) + chunk_specs(prep(bwd)),
        out_specs=[pl.BlockSpec((1, GDN_CHUNK, GDN_WIDTH), apply_(fwd)),
                   pl.BlockSpec((1, GDN_CHUNK, GDN_WIDTH), apply_(bwd))],
        out_shape=[sds((bsz, n_all, GDN_WIDTH), F32)] * 2,
        scratch_shapes=[pltpu.VMEM(per_stream + (GDN_HEAD_DIM, GDN_HEAD_DIM), F32),
                        pltpu.VMEM(per_stream + (GDN_CHUNK, GDN_HEAD_DIM), F32),
                        pltpu.VMEM(per_stream + (2, 2 * GDN_CHUNK, GDN_CHUNK), BF16),
                        pltpu.VMEM(per_stream + (SUBLANES, GDN_CHUNK), F32)],
        compiler_params=_params("parallel", "arbitrary"),
        name="gdn",
    )(q, k, v, bg, q, k, v, bg)

    half = LRU_WIDTH // 2
    r = pl.pallas_call(
        functools.partial(_lru_kernel, nt=nt),
        grid=(bsz, 2),
        in_specs=[pl.BlockSpec((1, n_all, half), lambda b, c: (b, 0, c)),
                  pl.BlockSpec((2, 2, 1, half, half), lambda b, c: (0, 0, c, 0, 0)),
                  pl.BlockSpec((2, 2, 1, 1, half), lambda b, c: (0, 0, c, 0, 0)),
                  pl.BlockSpec((2, half), lambda b, c: (0, c))],
        out_specs=pl.BlockSpec((1, n_all, half), lambda b, c: (b, 0, c)),
        out_shape=sds((bsz, n_all, LRU_WIDTH), F32),
        scratch_shapes=[pltpu.VMEM((TM, half), F32), pltpu.VMEM((TM, half), F32)],
        compiler_params=_params("parallel", "parallel"),
        name="rg_lru",
    )(xl, wg, lru_gate_b.reshape(2, 2, 2, 1, half), lru_lambda)

    off = 1 if last else 0
    tok_in = lambda w: pl.BlockSpec((1, TM, w), lambda b, i: (b, i + off, 0))
    return pl.pallas_call(
        _rec_out_kernel,
        grid=(bsz, nt - off),
        in_specs=[tok_in(GDN_WIDTH), tok_in(GDN_WIDTH), tok_in(LRU_WIDTH), tok_in(2 * 512), full((1, LANES)),
                  full((GDN_WIDTH + LRU_WIDTH, d)), tok_in(d), _mod_spec(off), full((1, d)), full((1, d))],
        out_specs=tok(d),
        out_shape=sds((bsz, n_all - off * TM, d), F32),
        compiler_params=_params("parallel", "parallel"),
        name="rec_out",
    )(outs[0], outs[1], r, sg, gdn_norm[None, :], w_out.astype(BF16), xs, msel, ln_g[None, :], ln_b[None, :])


def kernel(x, c, ctx, c_ctx, mod_w, mod_b, ln_g, ln_b, att_w_in, att_w_out, mla_q_norm, mla_w_uq, mla_kv_norm, mla_w_ukv, gqa_q_norm, gqa_k_norm, rec_w_in, rec_w_out, gdn_conv_w, gdn_a_log, gdn_dt_bias, gdn_norm, lru_conv_w, lru_conv_b, lru_gate_w, lru_gate_b, lru_lambda):
    bsz, n_seq, _ = x.shape
    assert ctx.shape[1] == CTX_LEN == TM and n_seq % TM == 0 and bsz < SUBLANES
    rope_a, rope_b = _rope_tables(n_seq)
    mods = _modulation(c, c_ctx, mod_w, mod_b)
    xs = jnp.concatenate([ctx, x], 1)
    for layer in range(DEPTH):
        last = layer == DEPTH - 1
        li = layer // 2
        msel = _mod_select(mods[layer], bsz)
        if layer % 2 == 0:
            xs = _attention_layer(xs, msel, att_w_in[li], att_w_out[li], mla_q_norm[li], mla_w_uq[li],
                                  mla_kv_norm[li], mla_w_ukv[li], gqa_q_norm[li], gqa_k_norm[li],
                                  rope_a, rope_b, ln_g[layer], ln_b[layer], last)
        else:
            xs = _recurrent_layer(xs, msel, rec_w_in[li], rec_w_out[li], gdn_conv_w[li], gdn_a_log[li],
                                  gdn_dt_bias[li], gdn_norm[li], lru_conv_w[li], lru_conv_b[li],
                                  lru_gate_w[li], lru_gate_b[li], lru_lambda[li], ln_g[layer], ln_b[layer], last)
    return xs
```

```python
import functools

import numpy as np
import jax
import jax.numpy as jnp
from jax import lax
from jax.experimental import pallas as pl
from jax.experimental.pallas import tpu as pltpu

F32 = jnp.float32
BF16 = jnp.bfloat16

D_MODEL = 1024
DEPTH = 4
GRID_W = 64
CTX_LEN = 256
ROPE_THETA = 10000.0
EPS = 1e-6

MLA_HEADS = 8
MLA_Q_RANK = 256
MLA_KV_RANK = 128
MLA_NOPE = 64
MLA_ROPE = 32
MLA_V = 64
MLA_WIDTH = MLA_HEADS * MLA_V
GQA_HEADS = 8
GQA_KV_HEADS = 2
GQA_HEAD_DIM = 64
GQA_WIDTH = GQA_HEADS * GQA_HEAD_DIM
GDN_HEADS = 4
GDN_HEAD_DIM = 128
GDN_WIDTH = GDN_HEADS * GDN_HEAD_DIM
LRU_WIDTH = 512
LRU_BLOCKS = 8
LRU_BLOCK_W = LRU_WIDTH // LRU_BLOCKS
LRU_C = 8.0
DEEPNORM_ALPHA = (2 * DEPTH) ** 0.25

LANES = 128
SUBLANES = 8
TM = 256
GDN_CHUNK = 128
GDN_DIAG = 16
KEY_CHUNK = 512
PV_LAG = 4
LOG2_E = 1.4426950408889634
VMEM_LIMIT = 48 * 1024 * 1024

ATT_COLS = 2560
REC_CONV_COLS = 3 * GDN_WIDTH + LRU_WIDTH
REC_COLS = REC_CONV_COLS + 2 * 512 + LANES


def _params(*sem):
    return pltpu.CompilerParams(dimension_semantics=sem, vmem_limit_bytes=VMEM_LIMIT)


def _bdot(a, b):
    return jnp.dot(a.astype(BF16), b.astype(BF16), preferred_element_type=F32)


def _split_bf16(a, parts):
    out = []
    for _ in range(parts):
        hi = a.astype(BF16)
        out.append(hi)
        a = a - hi.astype(F32)
    return out


def _dot_split(a, b):
    a_hi, a_lo = _split_bf16(a, 2)
    b_hi, b_lo = _split_bf16(b, 2)
    d = functools.partial(jnp.dot, preferred_element_type=F32)
    return d(a_hi, b_hi) + (d(a_lo, b_hi) + d(a_hi, b_lo))


def _silu(x):
    return x * jax.nn.sigmoid(x)


def _lane_iota(shape):
    return lax.broadcasted_iota(jnp.int32, shape, len(shape) - 1)


def _swap_groups(x, n):
    fwd = pltpu.roll(x, LANES - n, 1)
    bwd = pltpu.roll(x, n, 1)
    return jnp.where((_lane_iota(x.shape) % (2 * n)) < n, fwd, bwd)


def _rope(x, cos, sin, n):
    return x * cos + _swap_groups(x, n) * sin


def _layer_norm(z, g, b):
    mu = jnp.mean(z, -1, keepdims=True)
    zc = z - mu
    var = jnp.mean(zc * zc, -1, keepdims=True)
    return zc * lax.rsqrt(var + EPS) * g + b


def _mod_kernel(c_ref, w_ref, b_ref, o_ref):
    c = c_ref[...]
    o_ref[0] = _dot_split(_silu(c), w_ref[0]) + b_ref[0]


def _modulation(c, c_ctx, mod_w, mod_b):
    bsz = c.shape[0]
    rows = jnp.concatenate([c, c_ctx[None, :], jnp.zeros((SUBLANES - bsz - 1, D_MODEL), F32)], 0)
    tn = 1024
    return pl.pallas_call(
        _mod_kernel,
        grid=(DEPTH, 3 * D_MODEL // tn),
        in_specs=[pl.BlockSpec((SUBLANES, D_MODEL), lambda l, n: (0, 0)),
                  pl.BlockSpec((1, D_MODEL, tn), lambda l, n: (l, 0, n)),
                  pl.BlockSpec((1, 1, tn), lambda l, n: (l, 0, n))],
        out_specs=pl.BlockSpec((1, SUBLANES, tn), lambda l, n: (l, 0, n)),
        out_shape=jax.ShapeDtypeStruct((DEPTH, SUBLANES, 3 * D_MODEL), F32),
        compiler_params=_params("parallel", "parallel"),
        name="modulation",
    )(rows, mod_w, mod_b.reshape(DEPTH, 1, 3 * D_MODEL))


def _mod_select(mods_l, bsz):
    m3 = mods_l.reshape(SUBLANES, 3, D_MODEL)
    ctx_m = jnp.broadcast_to(m3[bsz], (bsz, 3, D_MODEL))
    sel = jnp.stack([ctx_m, m3[:bsz]], 1)
    return jnp.pad(sel, ((0, 0), (0, 0), (0, SUBLANES - 3), (0, 0)))


def _mod_spec(off=0):
    return pl.BlockSpec((1, 1, SUBLANES, D_MODEL), lambda b, i: (b, jnp.minimum(i + off, 1), 0, 0))


def _att_proj_kernel(x_ref, m_ref, win_ref, wuq_ref, wk_ref, wv_ref, nq_ref, nkv_ref, nqb_ref, nkb_ref,
                     ra_ref, rat_ref, rb_ref, qa_ref, ka_ref, va_ref, qb_ref, kb_ref, vb_ref, sg_ref):
    shift = m_ref[0, 0, 0:1, :]
    scale = m_ref[0, 0, 1:2, :]
    h = x_ref[0] * (1.0 + scale) + shift
    p = _bdot(h, win_ref[...])
    cq, ckv, kr = p[:, 0:256], p[:, 256:384], p[:, 384:512]
    gates = p[:, 512:1536]
    qb, kb, vb = p[:, 1536:2048], p[:, 2048:2304], p[:, 2304:2560]
    sg_ref[0] = _silu(gates).astype(BF16)

    def rms(x):
        return x * lax.rsqrt(jnp.mean(x * x, -1, keepdims=True) + EPS)

    qa_t = _bdot(wuq_ref[...], (rms(cq) * nq_ref[...]).T)
    ckvn = rms(ckv) * nkv_ref[...]
    kn = _bdot(ckvn, wk_ref[...])
    va_t = _bdot(wv_ref[...], ckvn.T)
    for j in range(MLA_HEADS // 2):
        va_ref[0, j] = va_t[j * LANES:(j + 1) * LANES].astype(BF16)
    cos_a, sin_a = ra_ref[0], ra_ref[1]
    kr = _rope(kr, cos_a, sin_a, MLA_ROPE // 4)
    cos_t, sin_t = rat_ref[0], rat_ref[1]
    q_scale = LOG2_E * (MLA_NOPE + MLA_ROPE) ** -0.5
    r0, n8 = MLA_NOPE, MLA_ROPE // 4
    for hd in range(MLA_HEADS):
        sl = slice(hd * LANES, (hd + 1) * LANES)
        qt = qa_t[sl]
        swapped = jnp.concatenate([qt[:r0], qt[r0 + n8:r0 + 2 * n8], qt[r0:r0 + n8], qt[r0 + 3 * n8:r0 + 4 * n8],
                                   qt[r0 + 2 * n8:r0 + 3 * n8], qt[r0 + 4 * n8:]], 0)
        qa_ref[0, hd] = ((qt * cos_t + swapped * sin_t) * q_scale).astype(BF16)
        ka_ref[0, hd] = (kn[:, sl] + kr).astype(BF16)

    cos_b, sin_b = rb_ref[0], rb_ref[1]
    low = _lane_iota((TM, LANES)) < GQA_HEAD_DIM
    for c in range(GQA_HEADS // 2):
        sl = slice(c * LANES, (c + 1) * LANES)
        xq = qb[:, sl]
        sq = xq * xq
        s_lo = jnp.sum(jnp.where(low, sq, 0.0), -1, keepdims=True)
        s_hi = jnp.sum(jnp.where(low, 0.0, sq), -1, keepdims=True)
        ms = jnp.where(low, s_lo, s_hi) * (1.0 / GQA_HEAD_DIM)
        xn = xq * lax.rsqrt(ms + EPS) * nqb_ref[:, sl]
        qt = (_rope(xn, cos_b, sin_b, GQA_HEAD_DIM // 4) * (LOG2_E * GQA_HEAD_DIM ** -0.5)).T
        top = lax.broadcasted_iota(jnp.int32, qt.shape, 0) < GQA_HEAD_DIM
        qb_ref[0, 2 * c] = jnp.where(top, qt, 0.0).astype(BF16)
        qb_ref[0, 2 * c + 1] = jnp.where(top, 0.0, qt).astype(BF16)
    for g in range(GQA_KV_HEADS):
        sl = slice(g * LANES, (g + 1) * LANES)
        kb_ref[0, g] = _rope(rms(kb[:, sl]) * nkb_ref[...], cos_b, sin_b, GQA_HEAD_DIM // 4).astype(BF16)
        vb_ref[0, g] = vb[:, sl].T.astype(BF16)


def _fold_rows(op, x, acc):
    for r in range(x.shape[0] // SUBLANES):
        acc = op(acc, x[r * SUBLANES:(r + 1) * SUBLANES])
    return acc


def _attend_heads(qtf, kf, vtf, s_refs, n_heads, nk):
    n_chunks = max(nk // KEY_CHUNK, 1)
    bounds = [i * KEY_CHUNK for i in range(n_chunks)] + [nk]
    chunks = [(bounds[i], bounds[i + 1] - bounds[i]) for i in range(n_chunks)]

    def score(h, c0, n, mt):
        s = jnp.dot(kf(h, c0, n), qtf(h), preferred_element_type=F32)
        s_refs[h % 2][c0:c0 + n, :] = s
        return _fold_rows(jnp.maximum, s, mt)

    neg = jnp.full((SUBLANES, TM), -jnp.inf, F32)
    zero = jnp.zeros((SUBLANES, TM), F32)
    acc = [None] * n_heads
    col_sum = [zero] * n_heads
    pending = []

    def pv_oldest():
        h, p, c0, n = pending.pop(0)
        o = jnp.dot(vtf(h, c0, n), p, preferred_element_type=F32)
        acc[h] = o if acc[h] is None else acc[h] + o

    mt = neg
    for c0, n in chunks:
        mt = score(0, c0, n, mt)
    for h in range(n_heads):
        m = jnp.max(mt, 0, keepdims=True)
        mt = neg
        for c0, n in chunks:
            p = jnp.exp2(s_refs[h % 2][c0:c0 + n, :] - m)
            col_sum[h] = _fold_rows(jnp.add, p, col_sum[h])
            pending.append((h, p.astype(BF16), c0, n))
            if h + 1 < n_heads:
                mt = score(h + 1, c0, n, mt)
            if len(pending) > PV_LAG:
                pv_oldest()
    while pending:
        pv_oldest()
    return [acc[h] / jnp.sum(col_sum[h], 0, keepdims=True) for h in range(n_heads)]


def _store_head_pairs(outs, sg_ref, o_ref):
    for j in range(len(outs) // 2):
        sl = slice(j * LANES, (j + 1) * LANES)
        pair = jnp.concatenate([outs[2 * j], outs[2 * j + 1]], 0).T
        o_ref[0, :, sl] = (pair * sg_ref[0, :, sl].astype(F32)).astype(BF16)


def _attn_kernel(qt_ref, k_ref, vt_ref, sg_ref, o_ref, sa_ref, sb_ref, *, n_all, n_heads, dv, k_of, v_of):
    def vt_chunk(h, c0, n):
        blk, r0 = v_of(h)
        return vt_ref[0, blk, r0:r0 + dv, c0:c0 + n]

    def run(nk):
        outs = _attend_heads(lambda h: qt_ref[0, h], lambda h, c0, n: k_ref[0, k_of(h), c0:c0 + n, :],
                             vt_chunk, (sa_ref, sb_ref), n_heads, nk)
        _store_head_pairs(outs, sg_ref, o_ref)

    @pl.when(pl.program_id(1) == 0)
    def _():
        run(CTX_LEN)

    @pl.when(pl.program_id(1) > 0)
    def _():
        run(n_all)


def _att_out_kernel(ma_ref, mb_ref, w_ref, x_ref, m_ref, g_ref, b_ref, o_ref):
    y = (jnp.dot(ma_ref[0], w_ref[0:MLA_WIDTH, :], preferred_element_type=F32)
         + jnp.dot(mb_ref[0], w_ref[MLA_WIDTH:, :], preferred_element_type=F32))
    gate = m_ref[0, 0, 2:3, :]
    o_ref[0] = _layer_norm(DEEPNORM_ALPHA * x_ref[0] + gate * y, g_ref[...], b_ref[...])


def _rope_tables(n_seq):
    pos = np.arange(n_seq)
    row = (pos // GRID_W).astype(np.float32)[:, None]
    col = (pos % GRID_W).astype(np.float32)[:, None]

    def quarter(dim):
        half = dim // 2
        inv = (ROPE_THETA ** (-np.arange(0, half, 2, dtype=np.float32) / half)).astype(np.float32)
        return np.cos(row * inv), np.sin(row * inv), np.cos(col * inv), np.sin(col * inv)

    def with_ctx(cos, sin):
        cos = np.concatenate([np.ones((CTX_LEN, LANES), np.float32), cos], 0)
        sin = np.concatenate([np.zeros((CTX_LEN, LANES), np.float32), sin], 0)
        return jnp.asarray(np.stack([cos, sin], 0).astype(np.float32))

    cr, sr, cc, sc = quarter(MLA_ROPE)
    ones = lambda n: np.ones((n_seq, n), np.float32)
    cos_a = np.concatenate([ones(MLA_NOPE), cr, cr, cc, cc, ones(32)], 1)
    sin_a = np.concatenate([0 * ones(MLA_NOPE), -sr, sr, -sc, sc, 0 * ones(32)], 1)
    cr, sr, cc, sc = quarter(GQA_HEAD_DIM)
    cos_b = np.concatenate([cr, cr, cc, cc] * 2, 1)
    sin_b = np.concatenate([-sr, sr, -sc, sc] * 2, 1)
    return with_ctx(cos_a, sin_a), with_ctx(cos_b, sin_b)


def _att_weights(w_in, w_uq, w_ukv):
    d = D_MODEL
    o = np.cumsum((0, MLA_Q_RANK, MLA_KV_RANK, MLA_ROPE, MLA_WIDTH, GQA_WIDTH, 128, 128, GQA_WIDTH))
    cq, ckv, kr, ga, qb, kb, vb, gb = (w_in[:, o[i]:o[i + 1]] for i in range(8))
    z = lambda n: jnp.zeros((d, n), F32)
    dup = lambda w: jnp.concatenate([w[:, 0:64], w[:, 0:64], w[:, 64:128], w[:, 64:128]], 1)
    win = jnp.concatenate([cq, ckv, z(MLA_NOPE), kr, z(32), ga, gb, qb, dup(kb), dup(vb)], 1)
    wq = w_uq.reshape(MLA_Q_RANK, MLA_HEADS, MLA_NOPE + MLA_ROPE)
    wq = jnp.pad(wq, ((0, 0), (0, 0), (0, 32))).reshape(MLA_Q_RANK, MLA_HEADS * LANES)
    wkv = w_ukv.reshape(MLA_KV_RANK, MLA_HEADS, MLA_NOPE + MLA_V)
    wk = jnp.pad(wkv[:, :, :MLA_NOPE], ((0, 0), (0, 0), (0, 64))).reshape(MLA_KV_RANK, MLA_HEADS * LANES)
    wv = wkv[:, :, MLA_NOPE:].reshape(MLA_KV_RANK, MLA_WIDTH)
    return win.astype(BF16), wq.T.astype(BF16), wk.astype(BF16), wv.T.astype(BF16)


def _attention_layer(xs, msel, w_in, w_out, q_norm_a, w_uq, kv_norm_a, w_ukv, q_norm_b, k_norm_b,
                     rope_a, rope_b, ln_g, ln_b, last):
    bsz, n_all, d = xs.shape
    nt = n_all // TM
    win, wq, wk, wv = _att_weights(w_in, w_uq, w_ukv)
    full = lambda shape: pl.BlockSpec(shape, lambda b, i: (0,) * len(shape))
    tok = lambda w: pl.BlockSpec((1, TM, w), lambda b, i: (b, i, 0))
    head = lambda n: pl.BlockSpec((1, n, TM, LANES), lambda b, i: (b, 0, i, 0))
    head_t = lambda n: pl.BlockSpec((1, n, LANES, TM), lambda b, i: (b, 0, 0, i))
    rope = pl.BlockSpec((2, TM, LANES), lambda b, i: (0, i, 0))
    sds = jax.ShapeDtypeStruct
    qa, ka, va, qb, kb, vb, sg = pl.pallas_call(
        _att_proj_kernel,
        grid=(bsz, nt),
        in_specs=[tok(d), _mod_spec(), full((d, ATT_COLS)), full((MLA_HEADS * LANES, MLA_Q_RANK)),
                  full((MLA_KV_RANK, MLA_HEADS * LANES)), full((MLA_WIDTH, MLA_KV_RANK)),
                  full((1, MLA_Q_RANK)), full((1, MLA_KV_RANK)), full((1, GQA_WIDTH)), full((1, LANES)),
                  rope, pl.BlockSpec((2, LANES, TM), lambda b, i: (0, 0, i)), rope],
        out_specs=[head_t(MLA_HEADS), head(MLA_HEADS), head_t(MLA_HEADS // 2), head_t(GQA_HEADS),
                   head(GQA_KV_HEADS), head_t(GQA_KV_HEADS), tok(2 * 512)],
        out_shape=[sds((bsz, MLA_HEADS, LANES, n_all), BF16), sds((bsz, MLA_HEADS, n_all, LANES), BF16),
                   sds((bsz, MLA_HEADS // 2, LANES, n_all), BF16), sds((bsz, GQA_HEADS, LANES, n_all), BF16),
                   sds((bsz, GQA_KV_HEADS, n_all, LANES), BF16), sds((bsz, GQA_KV_HEADS, LANES, n_all), BF16),
                   sds((bsz, n_all, 2 * 512), BF16)],
        compiler_params=_params("parallel", "parallel"),
        name="att_proj",
    )(xs, msel, win, wq, wk, wv, q_norm_a[None, :], kv_norm_a[None, :],
      jnp.tile(q_norm_b, GQA_HEADS)[None, :], jnp.tile(k_norm_b, 2)[None, :],
      rope_a, jnp.swapaxes(rope_a, 1, 2), rope_b)

    whole = lambda n: pl.BlockSpec((1, n, n_all, LANES), lambda b, i: (b, 0, 0, 0))
    whole_t = lambda n: pl.BlockSpec((1, n, LANES, n_all), lambda b, i: (b, 0, 0, 0))
    mix_a = pl.pallas_call(
        functools.partial(_attn_kernel, n_all=n_all, n_heads=MLA_HEADS, dv=MLA_V, k_of=lambda h: h,
                          v_of=lambda h: (h // 2, (h % 2) * MLA_V)),
        grid=(bsz, nt),
        in_specs=[head_t(MLA_HEADS), whole(MLA_HEADS), whole_t(MLA_HEADS // 2), tok(MLA_WIDTH)],
        out_specs=tok(MLA_WIDTH),
        out_shape=sds((bsz, n_all, MLA_WIDTH), BF16),
        scratch_shapes=[pltpu.VMEM((n_all, TM), F32)] * 2,
        compiler_params=_params("parallel", "parallel"),
        name="mla_attention",
    )(qa, ka, va, sg)
    group = GQA_HEADS // GQA_KV_HEADS
    mix_b = pl.pallas_call(
        functools.partial(_attn_kernel, n_all=n_all, n_heads=GQA_HEADS, dv=GQA_HEAD_DIM, k_of=lambda h: h // group,
                          v_of=lambda h: (h // group, 0)),
        grid=(bsz, nt),
        in_specs=[head_t(GQA_HEADS), whole(GQA_KV_HEADS), whole_t(GQA_KV_HEADS),
                  pl.BlockSpec((1, TM, GQA_WIDTH), lambda b, i: (b, i, 1))],
        out_specs=tok(GQA_WIDTH),
        out_shape=sds((bsz, n_all, GQA_WIDTH), BF16),
        scratch_shapes=[pltpu.VMEM((n_all, TM), F32)] * 2,
        compiler_params=_params("parallel", "parallel"),
        name="gqa_attention",
    )(qb, kb, vb, sg)

    off = 1 if last else 0
    tok_in = lambda w: pl.BlockSpec((1, TM, w), lambda b, i: (b, i + off, 0))
    return pl.pallas_call(
        _att_out_kernel,
        grid=(bsz, nt - off),
        in_specs=[tok_in(MLA_WIDTH), tok_in(GQA_WIDTH), full((MLA_WIDTH + GQA_WIDTH, d)), tok_in(d),
                  _mod_spec(off), full((1, d)), full((1, d))],
        out_specs=tok(d),
        out_shape=sds((bsz, n_all - off * TM, d), F32),
        compiler_params=_params("parallel", "parallel"),
        name="att_out",
    )(mix_a, mix_b, w_out.astype(BF16), xs, msel, ln_g[None, :], ln_b[None, :])


def _rec_proj_kernel(x_ref, xp_ref, xn_ref, m_ref, win_ref, cw_ref, cb_ref, gp_ref,
                     q_ref, k_ref, v_ref, bg_ref, sg_ref, xl_ref, *, nt):
    i = pl.program_id(1)
    shift = m_ref[0, 0, 0:1, :]
    scale = m_ref[0, 0, 1:2, :]
    mod = lambda x: x * (1.0 + scale) + shift
    prev_ok = (i >= 2).astype(F32)
    next_ok = jnp.logical_and(i >= 1, i < nt - 1).astype(F32)
    h = mod(x_ref[0])
    hext = jnp.concatenate([mod(xp_ref[0]) * prev_ok, h, mod(xn_ref[0]) * next_ok], 0)
    pext = _bdot(hext, win_ref[:, 0:REC_CONV_COLS])
    rest = _bdot(h, win_ref[:, REC_CONV_COLS:])
    n_ext = TM + 2 * SUBLANES
    y = cw_ref[1:2, :] * pext[SUBLANES:SUBLANES + TM]
    for j in (0, 2, 3):
        y = y + cw_ref[j:j + 1, :] * pltpu.roll(pext, (1 - j) % n_ext, 0)[SUBLANES:SUBLANES + TM]
    qkv = _silu(y[:, 0:3 * GDN_WIDTH])
    xl_ref[0] = y[:, 3 * GDN_WIDTH:] + cb_ref[...]
    for hd in range(GDN_HEADS):
        sl = slice(hd * LANES, (hd + 1) * LANES)
        q = qkv[:, sl]
        k = qkv[:, GDN_WIDTH + hd * LANES:GDN_WIDTH + (hd + 1) * LANES]
        q_ref[0, :, sl] = q * (lax.rsqrt(jnp.sum(q * q, -1, keepdims=True) + EPS) * GDN_HEAD_DIM ** -0.5)
        k_ref[0, :, sl] = k * lax.rsqrt(jnp.sum(k * k, -1, keepdims=True) + EPS)
    v_ref[0] = qkv[:, 2 * GDN_WIDTH:]
    sg_ref[0] = _silu(rest[:, 0:2 * 512]).astype(BF16)
    ba = rest[:, 2 * 512:]
    t = ba + gp_ref[1:2, :]
    softplus = jnp.maximum(t, 0.0) + jnp.log(1.0 + jnp.exp(-jnp.abs(t)))
    decay = -jnp.exp(gp_ref[0:1, :]) * softplus
    bg_ref[0] = jnp.where(_lane_iota(ba.shape) < 2 * GDN_HEADS, jax.nn.sigmoid(ba), decay)


def _each(f, *lists):
    return [f(*args) for args in zip(*lists)]


def _dots(xs, ys):
    return _each(_bdot, xs, ys)


def _stacked_dots(tops, bots, ys):
    n = tops[0].shape[0]
    both = [_bdot(jnp.concatenate([t, b], 0), y) for t, b, y in zip(tops, bots, ys)]
    return [o[:n] for o in both], [o[n:] for o in both]


def _tri_inverse(ms, strict_blocks, eye):
    mds = [jnp.where(strict_blocks, 0.0, m) for m in ms]
    ls = [jnp.where(strict_blocks, m, 0.0) for m in ms]
    add = lambda xs, ys: _each(lambda x, y: x + y, xs, ys)
    ps = [eye - md for md in mds]
    pws = _dots(mds, mds)
    for _ in range(2):
        pws, pn = _stacked_dots(pws, ps, pws)
        ps = add(ps, pn)
    dinvs = add(ps, _dots(ps, pws))
    es = _dots(dinvs, ls)
    qs = [eye - e for e in es]
    pws = _dots(es, es)
    pws, qn = _stacked_dots(pws, qs, pws)
    qs = add(qs, qn)
    qs = add(qs, _dots(qs, pws))
    return _dots(qs, dinvs)


def _gdn_kernel(*refs):
    c = GDN_CHUNK
    ins, outs = (refs[0:4], refs[4:8]), refs[8:10]
    s_ref, u_ref, ops_ref, dec_ref = refs[10:14]

    @pl.when(pl.program_id(1) == 0)
    def _():
        for ref in (s_ref, u_ref, ops_ref, dec_ref):
            ref[...] = jnp.zeros_like(ref)

    streams = [(d, hd) for d in range(2) for hd in range(GDN_HEADS)]
    lanes = lambda hd: slice(hd * LANES, (hd + 1) * LANES)
    ss = [s_ref[d, hd] for d, hd in streams]
    on_state = _dots([ops_ref[d, hd, 0] for d, hd in streams], ss)
    v_news = _each(lambda u, ws: u - ws[:c], [u_ref[d, hd] for d, hd in streams], on_state)
    on_new = _dots([ops_ref[d, hd, 1] for d, hd in streams], v_news)
    for (d, hd), s, a, b in zip(streams, ss, on_state, on_new):
        s_ref[d, hd] = s * dec_ref[d, hd, 0:1, :] + b[c:]
        outs[d][0, :, lanes(hd)] = a[c:] + b[:c]

    row_i = lax.broadcasted_iota(jnp.int32, (c, c), 0)
    col_i = lax.broadcasted_iota(jnp.int32, (c, c), 1)
    eye = (row_i == col_i).astype(F32)
    strict_blocks = (row_i // GDN_DIAG) != (col_i // GDN_DIAG)
    incls = [row_i >= col_i, row_i <= col_i]
    stricts = [row_i > col_i, row_i < col_i]
    lasts = [c - 1, 0]
    bgs = [ins[d][3][0] for d in range(2)]

    qs = [ins[d][0][0, :, lanes(hd)] for d, hd in streams]
    ks = [ins[d][1][0, :, lanes(hd)] for d, hd in streams]
    vs = [ins[d][2][0, :, lanes(hd)] for d, hd in streams]
    betas = [bgs[d][:, d * GDN_HEADS + hd:d * GDN_HEADS + hd + 1] for d, hd in streams]
    gs = [jnp.broadcast_to(bgs[d][:, (2 + d) * GDN_HEADS + hd:(2 + d) * GDN_HEADS + hd + 1], (c, c))
          for d, hd in streams]
    gcums = []
    for d in range(2):
        parts = [_split_bf16(g, 3) for g, (sd, _) in zip(gs, streams) if sd == d]
        wide = jnp.dot(incls[d].astype(BF16), jnp.concatenate([p for ps in parts for p in ps], 1),
                       preferred_element_type=F32)
        gcums += [wide[:, (3 * i) * c:(3 * i + 1) * c] + wide[:, (3 * i + 1) * c:(3 * i + 2) * c]
                  + wide[:, (3 * i + 2) * c:(3 * i + 3) * c] for i in range(GDN_HEADS)]
    gcum_ts = [g.T for g in gcums]
    g_lasts = [g[lasts[d]:lasts[d] + 1, :] for g, (d, _) in zip(gcums, streams)]
    decays = [jnp.exp(jnp.where(incls[d], g - gt, -1e30)) for g, gt, (d, _) in zip(gcums, gcum_ts, streams)]
    es = [jnp.exp(g) for g in gcums]
    kts = [k.T for k in ks]
    kbs = _each(lambda k, b: k * b, ks, betas)
    kks, qks = _stacked_dots(kbs, qs, kts)
    ms = [jnp.where(stricts[d], kk * dec, 0.0) for kk, dec, (d, _) in zip(kks, decays, streams)]
    attns = _each(lambda a, dec: a * dec, qks, decays)
    ts = _tri_inverse(ms, strict_blocks, eye)
    sols = _each(lambda t, v, b, kb, e: _bdot(t, jnp.concatenate([v * b, kb * e], 1)), ts, vs, betas, kbs, es)
    for (d, hd), sol, attn, q, e, kt, gl, gt in zip(streams, sols, attns, qs, es, kts, g_lasts, gcum_ts):
        u_ref[d, hd] = sol[:, :LANES]
        ops_ref[d, hd, 0, 0:c] = sol[:, LANES:].astype(BF16)
        ops_ref[d, hd, 0, c:2 * c] = (q * e).astype(BF16)
        ops_ref[d, hd, 1, 0:c] = attn.astype(BF16)
        ops_ref[d, hd, 1, c:2 * c] = (kt * jnp.exp(gl - gt)).astype(BF16)
        dec_ref[d, hd] = jnp.broadcast_to(jnp.exp(gl), (SUBLANES, c))


def _lru_kernel(xl_ref, wg_ref, bgate_ref, lam_ref, o_ref, a_ref, b_ref, *, nt):
    w = xl_ref.shape[-1]
    rows = lax.broadcasted_iota(jnp.int32, (SUBLANES, w), 0)
    groups = TM // SUBLANES
    for d in range(2):
        rev = d == 1
        lam = lam_ref[d:d + 1, :]
        log_sig = jnp.minimum(lam, 0.0) - jnp.log(1.0 + jnp.exp(-jnp.abs(lam)))

        def tile_body(s, carry):
            if rev:
                t = jnp.where(s == 0, 0, nt - s)
            else:
                t = s
            base = pl.multiple_of(t * TM, TM)
            x = xl_ref[0, pl.ds(base, TM), :]
            xb = x.astype(BF16)
            sigmoid = lambda t: 0.5 * jnp.tanh(0.5 * t) + 0.5
            r = sigmoid(jnp.dot(xb, wg_ref[d, 0, 0], preferred_element_type=F32) + bgate_ref[d, 0, 0])
            gi = sigmoid(jnp.dot(xb, wg_ref[d, 1, 0], preferred_element_type=F32) + bgate_ref[d, 1, 0])
            a = jnp.exp(LRU_C * r * log_sig)
            a_ref[...] = a
            b_ref[...] = jnp.sqrt(1.0 - a * a) * (gi * x)

            def group_body(gidx, h_prev):
                gq = (groups - 1 - gidx) if rev else gidx
                off = pl.multiple_of(gq * SUBLANES, SUBLANES)
                a = a_ref[pl.ds(off, SUBLANES), :]
                bv = b_ref[pl.ds(off, SUBLANES), :]
                for sft in (1, 2, 4):
                    if rev:
                        ok = rows < SUBLANES - sft
                        a_s = pltpu.roll(a, SUBLANES - sft, 0)
                        b_s = pltpu.roll(bv, SUBLANES - sft, 0)
                    else:
                        ok = rows >= sft
                        a_s = pltpu.roll(a, sft, 0)
                        b_s = pltpu.roll(bv, sft, 0)
                    bv = jnp.where(ok, a * b_s + bv, bv)
                    a = jnp.where(ok, a * a_s, a)
                hcur = a * h_prev + bv
                dst = pl.ds(base + off, SUBLANES)
                if rev:
                    o_ref[0, dst, :] = o_ref[0, dst, :] + hcur
                    return jnp.broadcast_to(hcur[0:1, :], hcur.shape)
                o_ref[0, dst, :] = hcur
                return jnp.broadcast_to(hcur[SUBLANES - 1:SUBLANES, :], hcur.shape)

            return lax.fori_loop(0, groups, group_body, carry, unroll=4)

        lax.fori_loop(0, nt, tile_body, jnp.zeros((SUBLANES, w), F32))


def _rec_out_kernel(of_ref, ob_ref, r_ref, sg_ref, gn_ref, w_ref, x_ref, m_ref, g_ref, b_ref, o_ref):
    o = of_ref[0] + ob_ref[0]
    sg = sg_ref[0].astype(F32)
    y = jnp.dot((r_ref[0] * sg[:, GDN_WIDTH:]).astype(BF16), w_ref[GDN_WIDTH:, :], preferred_element_type=F32)
    for hd in range(GDN_HEADS):
        sl = slice(hd * LANES, (hd + 1) * LANES)
        oh = o[:, sl]
        og = oh * lax.rsqrt(jnp.mean(oh * oh, -1, keepdims=True) + EPS) * gn_ref[...] * sg[:, sl]
        y = y + jnp.dot(og.astype(BF16), w_ref[sl, :], preferred_element_type=F32)
    gate = m_ref[0, 0, 2:3, :]
    o_ref[0] = _layer_norm(DEEPNORM_ALPHA * x_ref[0] + gate * y, g_ref[...], b_ref[...])


def _rec_weights(w_in, gdn_conv_w, lru_conv_w, gdn_a_log, gdn_dt_bias, lru_gate_w):
    d = D_MODEL
    o = np.cumsum((0, 3 * GDN_WIDTH, GDN_WIDTH, 2 * GDN_HEADS, 2 * GDN_HEADS, LRU_WIDTH, LRU_WIDTH))
    qkv, z, b, a, xr, gr = (w_in[:, o[i]:o[i + 1]] for i in range(6))
    win = jnp.concatenate([qkv, xr, z, gr, b, a, jnp.zeros((d, LANES - 4 * GDN_HEADS), F32)], 1)
    conv_w = jnp.pad(jnp.concatenate([gdn_conv_w, lru_conv_w], 1), ((0, SUBLANES - 4), (0, 0)))
    pad = lambda p: jnp.pad(p.reshape(-1), (2 * GDN_HEADS, LANES - 4 * GDN_HEADS))
    gparams = jnp.pad(jnp.stack([pad(gdn_a_log), pad(gdn_dt_bias)], 0), ((0, SUBLANES - 2), (0, 0)))
    half = LRU_WIDTH // 2
    blocks = lru_gate_w.reshape(2, 2, 2, LRU_BLOCKS // 2, LRU_BLOCK_W, LRU_BLOCK_W)
    eye = jnp.eye(LRU_BLOCKS // 2, dtype=F32)
    wg = jnp.einsum('dghncm,nk->dghnckm', blocks, eye).reshape(2, 2, 2, half, half)
    return win.astype(BF16), conv_w, gparams, wg.astype(BF16)


def _recurrent_layer(xs, msel, w_in, w_out, gdn_conv_w, gdn_a_log, gdn_dt_bias, gdn_norm, lru_conv_w,
                     lru_conv_b, lru_gate_w, lru_gate_b, lru_lambda, ln_g, ln_b, last):
    bsz, n_all, d = xs.shape
    nt = n_all // TM
    win, conv_w, gparams, wg = _rec_weights(w_in, gdn_conv_w, lru_conv_w, gdn_a_log, gdn_dt_bias, lru_gate_w)
    full = lambda shape: pl.BlockSpec(shape, lambda b, i: (0,) * len(shape))
    tok = lambda w: pl.BlockSpec((1, TM, w), lambda b, i: (b, i, 0))
    sds = jax.ShapeDtypeStruct
    per = TM // SUBLANES
    n8 = n_all // SUBLANES
    q, k, v, bg, sg, xl = pl.pallas_call(
        functools.partial(_rec_proj_kernel, nt=nt),
        grid=(bsz, nt),
        in_specs=[tok(d),
                  pl.BlockSpec((1, SUBLANES, d), lambda b, i: (b, jnp.maximum(i * per - 1, 0), 0)),
                  pl.BlockSpec((1, SUBLANES, d), lambda b, i: (b, jnp.minimum((i + 1) * per, n8 - 1), 0)),
                  _mod_spec(), full((d, REC_COLS)), full((SUBLANES, REC_CONV_COLS)), full((1, LRU_WIDTH)),
                  full((SUBLANES, LANES))],
        out_specs=[tok(GDN_WIDTH), tok(GDN_WIDTH), tok(GDN_WIDTH), tok(LANES), tok(2 * 512), tok(LRU_WIDTH)],
        out_shape=[sds((bsz, n_all, GDN_WIDTH), F32)] * 3 + [sds((bsz, n_all, LANES), F32),
                   sds((bsz, n_all, 2 * 512), BF16), sds((bsz, n_all, LRU_WIDTH), F32)],
        compiler_params=_params("parallel", "parallel"),
        name="rec_proj",
    )(xs, xs, xs, msel, win, conv_w, lru_conv_b[None, :], gparams)

    nch = n_all // GDN_CHUNK
    nctx = CTX_LEN // GDN_CHUNK
    fwd = lambda s: s
    bwd = lambda s: jnp.where(s < nctx, nctx - 1 - s, nch - 1 + nctx - s)
    prep = lambda order: (lambda b, s: (b, order(jnp.minimum(s, nch - 1)), 0))
    apply_ = lambda order: (lambda b, s: (b, order(jnp.maximum(s - 1, 0)), 0))
    chunk_specs = lambda cmap: [pl.BlockSpec((1, GDN_CHUNK, GDN_WIDTH), cmap)] * 3 + [pl.BlockSpec((1, GDN_CHUNK, LANES), cmap)]
    per_stream = (2, GDN_HEADS)
    outs = pl.pallas_call(
        _gdn_kernel,
        grid=(bsz, nch + 1),
        in_specs=chunk_specs(prep(fwd)) + chunk_specs(prep(bwd)),
        out_specs=[pl.BlockSpec((1, GDN_CHUNK, GDN_WIDTH), apply_(fwd)),
                   pl.BlockSpec((1, GDN_CHUNK, GDN_WIDTH), apply_(bwd))],
        out_shape=[sds((bsz, n_all, GDN_WIDTH), F32)] * 2,
        scratch_shapes=[pltpu.VMEM(per_stream + (GDN_HEAD_DIM, GDN_HEAD_DIM), F32),
                        pltpu.VMEM(per_stream + (GDN_CHUNK, GDN_HEAD_DIM), F32),
                        pltpu.VMEM(per_stream + (2, 2 * GDN_CHUNK, GDN_CHUNK), BF16),
                        pltpu.VMEM(per_stream + (SUBLANES, GDN_CHUNK), F32)],
        compiler_params=_params("parallel", "arbitrary"),
        name="gdn",
    )(q, k, v, bg, q, k, v, bg)

    half = LRU_WIDTH // 2
    r = pl.pallas_call(
        functools.partial(_lru_kernel, nt=nt),
        grid=(bsz, 2),
        in_specs=[pl.BlockSpec((1, n_all, half), lambda b, c: (b, 0, c)),
                  pl.BlockSpec((2, 2, 1, half, half), lambda b, c: (0, 0, c, 0, 0)),
                  pl.BlockSpec((2, 2, 1, 1, half), lambda b, c: (0, 0, c, 0, 0)),
                  pl.BlockSpec((2, half), lambda b, c: (0, c))],
        out_specs=pl.BlockSpec((1, n_all, half), lambda b, c: (b, 0, c)),
        out_shape=sds((bsz, n_all, LRU_WIDTH), F32),
        scratch_shapes=[pltpu.VMEM((TM, half), F32), pltpu.VMEM((TM, half), F32)],
        compiler_params=_params("parallel", "parallel"),
        name="rg_lru",
    )(xl, wg, lru_gate_b.reshape(2, 2, 2, 1, half), lru_lambda)

    off = 1 if last else 0
    tok_in = lambda w: pl.BlockSpec((1, TM, w), lambda b, i: (b, i + off, 0))
    return pl.pallas_call(
        _rec_out_kernel,
        grid=(bsz, nt - off),
        in_specs=[tok_in(GDN_WIDTH), tok_in(GDN_WIDTH), tok_in(LRU_WIDTH), tok_in(2 * 512), full((1, LANES)),
                  full((GDN_WIDTH + LRU_WIDTH, d)), tok_in(d), _mod_spec(off), full((1, d)), full((1, d))],
        out_specs=tok(d),
        out_shape=sds((bsz, n_all - off * TM, d), F32),
        compiler_params=_params("parallel", "parallel"),
        name="rec_out",
    )(outs[0], outs[1], r, sg, gdn_norm[None, :], w_out.astype(BF16), xs, msel, ln_g[None, :], ln_b[None, :])


def kernel(x, c, ctx, c_ctx, mod_w, mod_b, ln_g, ln_b, att_w_in, att_w_out, mla_q_norm, mla_w_uq, mla_kv_norm, mla_w_ukv, gqa_q_norm, gqa_k_norm, rec_w_in, rec_w_out, gdn_conv_w, gdn_a_log, gdn_dt_bias, gdn_norm, lru_conv_w, lru_conv_b, lru_gate_w, lru_gate_b, lru_lambda):
    bsz, n_seq, _ = x.shape
    assert ctx.shape[1] == CTX_LEN == TM and n_seq % TM == 0 and bsz < SUBLANES
    rope_a, rope_b = _rope_tables(n_seq)
    mods = _modulation(c, c_ctx, mod_w, mod_b)
    xs = jnp.concatenate([ctx, x], 1)
    for layer in range(DEPTH):
        last = layer == DEPTH - 1
        li = layer // 2
        msel = _mod_select(mods[layer], bsz)
        if layer % 2 == 0:
            xs = _attention_layer(xs, msel, att_w_in[li], att_w_out[li], mla_q_norm[li], mla_w_uq[li],
                                  mla_kv_norm[li], mla_w_ukv[li], gqa_q_norm[li], gqa_k_norm[li],
                                  rope_a, rope_b, ln_g[layer], ln_b[layer], last)
        else:
            xs = _recurrent_layer(xs, msel, rec_w_in[li], rec_w_out[li], gdn_conv_w[li], gdn_a_log[li],
                                  gdn_dt_bias[li], gdn_norm[li], lru_conv_w[li], lru_conv_b[li],
                                  lru_gate_w[li], lru_gate_b[li], lru_lambda[li], ln_g[layer], ln_b[layer], last)
    return xs
```

```python
import functools

import numpy as np
import jax
import jax.numpy as jnp
from jax import lax
from jax.experimental import pallas as pl
from jax.experimental.pallas import tpu as pltpu

F32 = jnp.float32
BF16 = jnp.bfloat16

D_MODEL = 1024
DEPTH = 4
GRID_W = 64
CTX_LEN = 256
ROPE_THETA = 10000.0
EPS = 1e-6

MLA_HEADS = 8
MLA_Q_RANK = 256
MLA_KV_RANK = 128
MLA_NOPE = 64
MLA_ROPE = 32
MLA_V = 64
MLA_WIDTH = MLA_HEADS * MLA_V
GQA_HEADS = 8
GQA_KV_HEADS = 2
GQA_HEAD_DIM = 64
GQA_WIDTH = GQA_HEADS * GQA_HEAD_DIM
GDN_HEADS = 4
GDN_HEAD_DIM = 128
GDN_WIDTH = GDN_HEADS * GDN_HEAD_DIM
LRU_WIDTH = 512
LRU_BLOCKS = 8
LRU_BLOCK_W = LRU_WIDTH // LRU_BLOCKS
LRU_C = 8.0
DEEPNORM_ALPHA = (2 * DEPTH) ** 0.25

LANES = 128
SUBLANES = 8
TM = 256
GDN_CHUNK = 128
GDN_DIAG = 16
KEY_CHUNK = 512
PV_LAG = 4
LOG2_E = 1.4426950408889634
VMEM_LIMIT = 48 * 1024 * 1024

ATT_COLS = 2560
REC_CONV_COLS = 3 * GDN_WIDTH + LRU_WIDTH
REC_COLS = REC_CONV_COLS + 2 * 512 + LANES


def _params(*sem):
    return pltpu.CompilerParams(dimension_semantics=sem, vmem_limit_bytes=VMEM_LIMIT)


def _bdot(a, b):
    return jnp.dot(a.astype(BF16), b.astype(BF16), preferred_element_type=F32)


def _split_bf16(a, parts):
    out = []
    for _ in range(parts):
        hi = a.astype(BF16)
        out.append(hi)
        a = a - hi.astype(F32)
    return out


def _dot_split(a, b):
    a_hi, a_lo = _split_bf16(a, 2)
    b_hi, b_lo = _split_bf16(b, 2)
    d = functools.partial(jnp.dot, preferred_element_type=F32)
    return d(a_hi, b_hi) + (d(a_lo, b_hi) + d(a_hi, b_lo))


def _silu(x):
    return x * jax.nn.sigmoid(x)


def _lane_iota(shape):
    return lax.broadcasted_iota(jnp.int32, shape, len(shape) - 1)


def _swap_groups(x, n):
    fwd = pltpu.roll(x, LANES - n, 1)
    bwd = pltpu.roll(x, n, 1)
    return jnp.where((_lane_iota(x.shape) % (2 * n)) < n, fwd, bwd)


def _rope(x, cos, sin, n):
    return x * cos + _swap_groups(x, n) * sin


def _layer_norm(z, g, b):
    mu = jnp.mean(z, -1, keepdims=True)
    zc = z - mu
    var = jnp.mean(zc * zc, -1, keepdims=True)
    return zc * lax.rsqrt(var + EPS) * g + b


def _mod_kernel(c_ref, w_ref, b_ref, o_ref):
    c = c_ref[...]
    o_ref[0] = _dot_split(_silu(c), w_ref[0]) + b_ref[0]


def _modulation(c, c_ctx, mod_w, mod_b):
    bsz = c.shape[0]
    rows = jnp.concatenate([c, c_ctx[None, :], jnp.zeros((SUBLANES - bsz - 1, D_MODEL), F32)], 0)
    tn = 1024
    return pl.pallas_call(
        _mod_kernel,
        grid=(DEPTH, 3 * D_MODEL // tn),
        in_specs=[pl.BlockSpec((SUBLANES, D_MODEL), lambda l, n: (0, 0)),
                  pl.BlockSpec((1, D_MODEL, tn), lambda l, n: (l, 0, n)),
                  pl.BlockSpec((1, 1, tn), lambda l, n: (l, 0, n))],
        out_specs=pl.BlockSpec((1, SUBLANES, tn), lambda l, n: (l, 0, n)),
        out_shape=jax.ShapeDtypeStruct((DEPTH, SUBLANES, 3 * D_MODEL), F32),
        compiler_params=_params("parallel", "parallel"),
        name="modulation",
    )(rows, mod_w, mod_b.reshape(DEPTH, 1, 3 * D_MODEL))


def _mod_select(mods_l, bsz):
    m3 = mods_l.reshape(SUBLANES, 3, D_MODEL)
    ctx_m = jnp.broadcast_to(m3[bsz], (bsz, 3, D_MODEL))
    sel = jnp.stack([ctx_m, m3[:bsz]], 1)
    return jnp.pad(sel, ((0, 0), (0, 0), (0, SUBLANES - 3), (0, 0)))


def _mod_spec(off=0):
    return pl.BlockSpec((1, 1, SUBLANES, D_MODEL), lambda b, i: (b, jnp.minimum(i + off, 1), 0, 0))


def _token_tile(refs, n_streams):
    if n_streams == 1:
        return refs[0][0]
    return jnp.where(pl.program_id(1) == 0, refs[0][0], refs[1][0])


def _att_proj_kernel(*refs, n_streams):
    (m_ref, win_ref, wuq_ref, wk_ref, wv_ref, nq_ref, nkv_ref, nqb_ref, nkb_ref, ra_ref, rat_ref, rb_ref,
     qa_ref, ka_ref, va_ref, qb_ref, kb_ref, vb_ref, sg_ref) = refs[n_streams:]
    shift = m_ref[0, 0, 0:1, :]
    scale = m_ref[0, 0, 1:2, :]
    h = _token_tile(refs, n_streams) * (1.0 + scale) + shift
    p = _bdot(h, win_ref[...])
    cq, ckv, kr = p[:, 0:256], p[:, 256:384], p[:, 384:512]
    gates = p[:, 512:1536]
    qb, kb, vb = p[:, 1536:2048], p[:, 2048:2304], p[:, 2304:2560]
    sg_ref[0] = _silu(gates).astype(BF16)

    def rms(x):
        return x * lax.rsqrt(jnp.mean(x * x, -1, keepdims=True) + EPS)

    qa_t = _bdot(wuq_ref[...], (rms(cq) * nq_ref[...]).T)
    ckvn = rms(ckv) * nkv_ref[...]
    kn = _bdot(ckvn, wk_ref[...])
    va_t = _bdot(wv_ref[...], ckvn.T)
    for j in range(MLA_HEADS // 2):
        va_ref[0, j] = va_t[j * LANES:(j + 1) * LANES].astype(BF16)
    cos_a, sin_a = ra_ref[0], ra_ref[1]
    kr = _rope(kr, cos_a, sin_a, MLA_ROPE // 4)
    cos_t, sin_t = rat_ref[0], rat_ref[1]
    q_scale = LOG2_E * (MLA_NOPE + MLA_ROPE) ** -0.5
    r0, n8 = MLA_NOPE, MLA_ROPE // 4
    for hd in range(MLA_HEADS):
        sl = slice(hd * LANES, (hd + 1) * LANES)
        qt = qa_t[sl]
        swapped = jnp.concatenate([qt[:r0], qt[r0 + n8:r0 + 2 * n8], qt[r0:r0 + n8], qt[r0 + 3 * n8:r0 + 4 * n8],
                                   qt[r0 + 2 * n8:r0 + 3 * n8], qt[r0 + 4 * n8:]], 0)
        qa_ref[0, hd] = ((qt * cos_t + swapped * sin_t) * q_scale).astype(BF16)
        ka_ref[0, hd] = (kn[:, sl] + kr).astype(BF16)

    cos_b, sin_b = rb_ref[0], rb_ref[1]
    low = _lane_iota((TM, LANES)) < GQA_HEAD_DIM
    for c in range(GQA_HEADS // 2):
        sl = slice(c * LANES, (c + 1) * LANES)
        xq = qb[:, sl]
        sq = xq * xq
        s_lo = jnp.sum(jnp.where(low, sq, 0.0), -1, keepdims=True)
        s_hi = jnp.sum(jnp.where(low, 0.0, sq), -1, keepdims=True)
        ms = jnp.where(low, s_lo, s_hi) * (1.0 / GQA_HEAD_DIM)
        xn = xq * lax.rsqrt(ms + EPS) * nqb_ref[:, sl]
        qt = (_rope(xn, cos_b, sin_b, GQA_HEAD_DIM // 4) * (LOG2_E * GQA_HEAD_DIM ** -0.5)).T
        top = lax.broadcasted_iota(jnp.int32, qt.shape, 0) < GQA_HEAD_DIM
        qb_ref[0, 2 * c] = jnp.where(top, qt, 0.0).astype(BF16)
        qb_ref[0, 2 * c + 1] = jnp.where(top, 0.0, qt).astype(BF16)
    for g in range(GQA_KV_HEADS):
        sl = slice(g * LANES, (g + 1) * LANES)
        kb_ref[0, g] = _rope(rms(kb[:, sl]) * nkb_ref[...], cos_b, sin_b, GQA_HEAD_DIM // 4).astype(BF16)
        vb_ref[0, g] = vb[:, sl].T.astype(BF16)


def _fold_rows(op, x, acc):
    for r in range(x.shape[0] // SUBLANES):
        acc = op(acc, x[r * SUBLANES:(r + 1) * SUBLANES])
    return acc


def _attend_heads(qtf, kf, vtf, s_refs, n_heads, nk):
    n_chunks = max(nk // KEY_CHUNK, 1)
    bounds = [i * KEY_CHUNK for i in range(n_chunks)] + [nk]
    chunks = [(bounds[i], bounds[i + 1] - bounds[i]) for i in range(n_chunks)]

    def score(h, c0, n, mt):
        s = jnp.dot(kf(h, c0, n), qtf(h), preferred_element_type=F32)
        s_refs[h % 2][c0:c0 + n, :] = s
        return _fold_rows(jnp.maximum, s, mt)

    neg = jnp.full((SUBLANES, TM), -jnp.inf, F32)
    zero = jnp.zeros((SUBLANES, TM), F32)
    acc = [None] * n_heads
    col_sum = [zero] * n_heads
    pending = []

    def pv_oldest():
        h, p, c0, n = pending.pop(0)
        o = jnp.dot(vtf(h, c0, n), p, preferred_element_type=F32)
        acc[h] = o if acc[h] is None else acc[h] + o

    mt = neg
    for c0, n in chunks:
        mt = score(0, c0, n, mt)
    for h in range(n_heads):
        m = jnp.max(mt, 0, keepdims=True)
        mt = neg
        for c0, n in chunks:
            p = jnp.exp2(s_refs[h % 2][c0:c0 + n, :] - m)
            col_sum[h] = _fold_rows(jnp.add, p, col_sum[h])
            pending.append((h, p.astype(BF16), c0, n))
            if h + 1 < n_heads:
                mt = score(h + 1, c0, n, mt)
            if len(pending) > PV_LAG:
                pv_oldest()
    while pending:
        pv_oldest()
    return [acc[h] / jnp.sum(col_sum[h], 0, keepdims=True) for h in range(n_heads)]


def _store_head_pairs(outs, sg_ref, o_ref):
    for j in range(len(outs) // 2):
        sl = slice(j * LANES, (j + 1) * LANES)
        pair = jnp.concatenate([outs[2 * j], outs[2 * j + 1]], 0).T
        o_ref[0, :, sl] = (pair * sg_ref[0, :, sl].astype(F32)).astype(BF16)


def _attn_kernel(qt_ref, k_ref, vt_ref, sg_ref, o_ref, sa_ref, sb_ref, *, n_all, n_heads, dv, k_of, v_of):
    def vt_chunk(h, c0, n):
        blk, r0 = v_of(h)
        return vt_ref[0, blk, r0:r0 + dv, c0:c0 + n]

    def run(nk):
        outs = _attend_heads(lambda h: qt_ref[0, h], lambda h, c0, n: k_ref[0, k_of(h), c0:c0 + n, :],
                             vt_chunk, (sa_ref, sb_ref), n_heads, nk)
        _store_head_pairs(outs, sg_ref, o_ref)

    @pl.when(pl.program_id(1) == 0)
    def _():
        run(CTX_LEN)

    @pl.when(pl.program_id(1) > 0)
    def _():
        run(n_all)


def _att_out_kernel(*refs, n_streams):
    ma_ref, mb_ref, w_ref, m_ref, g_ref, b_ref, o_ref = refs[n_streams:]
    y = (jnp.dot(ma_ref[0], w_ref[0:MLA_WIDTH, :], preferred_element_type=F32)
         + jnp.dot(mb_ref[0], w_ref[MLA_WIDTH:, :], preferred_element_type=F32))
    gate = m_ref[0, 0, 2:3, :]
    o_ref[0] = _layer_norm(DEEPNORM_ALPHA * _token_tile(refs, n_streams) + gate * y, g_ref[...], b_ref[...])


def _rope_tables(n_seq):
    pos = np.arange(n_seq)
    row = (pos // GRID_W).astype(np.float32)[:, None]
    col = (pos % GRID_W).astype(np.float32)[:, None]

    def quarter(dim):
        half = dim // 2
        inv = (ROPE_THETA ** (-np.arange(0, half, 2, dtype=np.float32) / half)).astype(np.float32)
        return np.cos(row * inv), np.sin(row * inv), np.cos(col * inv), np.sin(col * inv)

    def with_ctx(cos, sin):
        cos = np.concatenate([np.ones((CTX_LEN, LANES), np.float32), cos], 0)
        sin = np.concatenate([np.zeros((CTX_LEN, LANES), np.float32), sin], 0)
        return jnp.asarray(np.stack([cos, sin], 0).astype(np.float32))

    cr, sr, cc, sc = quarter(MLA_ROPE)
    ones = lambda n: np.ones((n_seq, n), np.float32)
    cos_a = np.concatenate([ones(MLA_NOPE), cr, cr, cc, cc, ones(32)], 1)
    sin_a = np.concatenate([0 * ones(MLA_NOPE), -sr, sr, -sc, sc, 0 * ones(32)], 1)
    cr, sr, cc, sc = quarter(GQA_HEAD_DIM)
    cos_b = np.concatenate([cr, cr, cc, cc] * 2, 1)
    sin_b = np.concatenate([-sr, sr, -sc, sc] * 2, 1)
    return with_ctx(cos_a, sin_a), with_ctx(cos_b, sin_b)


def _att_weights(w_in, w_uq, w_ukv):
    d = D_MODEL
    o = np.cumsum((0, MLA_Q_RANK, MLA_KV_RANK, MLA_ROPE, MLA_WIDTH, GQA_WIDTH, 128, 128, GQA_WIDTH))
    cq, ckv, kr, ga, qb, kb, vb, gb = (w_in[:, o[i]:o[i + 1]] for i in range(8))
    z = lambda n: jnp.zeros((d, n), F32)
    dup = lambda w: [w[:, 0:64], w[:, 0:64], w[:, 64:128], w[:, 64:128]]
    pieces = [cq, ckv, z(MLA_NOPE), kr, z(32), ga, gb, qb] + dup(kb) + dup(vb)
    win = jnp.concatenate([p.astype(BF16) for p in pieces], 1)
    wq = w_uq.reshape(MLA_Q_RANK, MLA_HEADS, MLA_NOPE + MLA_ROPE)
    wq = jnp.pad(wq, ((0, 0), (0, 0), (0, 32))).reshape(MLA_Q_RANK, MLA_HEADS * LANES)
    wkv = w_ukv.reshape(MLA_KV_RANK, MLA_HEADS, MLA_NOPE + MLA_V)
    wk = jnp.pad(wkv[:, :, :MLA_NOPE], ((0, 0), (0, 0), (0, 64))).reshape(MLA_KV_RANK, MLA_HEADS * LANES)
    wv = wkv[:, :, MLA_NOPE:].reshape(MLA_KV_RANK, MLA_WIDTH)
    return win, wq.T.astype(BF16), wk.astype(BF16), wv.T.astype(BF16)


def _attention_layer(xs, msel, w_in, w_out, q_norm_a, w_uq, kv_norm_a, w_ukv, q_norm_b, k_norm_b,
                     rope_a, rope_b, ln_g, ln_b, last):
    streams = xs if isinstance(xs, tuple) else (xs,)
    bsz, d = streams[-1].shape[0], streams[-1].shape[2]
    n_all = sum(a.shape[1] for a in streams)
    nt = n_all // TM
    win, wq, wk, wv = _att_weights(w_in, w_uq, w_ukv)
    full = lambda shape: pl.BlockSpec(shape, lambda b, i: (0,) * len(shape))
    tok = lambda w: pl.BlockSpec((1, TM, w), lambda b, i: (b, i, 0))
    head = lambda n: pl.BlockSpec((1, n, TM, LANES), lambda b, i: (b, 0, i, 0))
    head_t = lambda n: pl.BlockSpec((1, n, LANES, TM), lambda b, i: (b, 0, 0, i))
    rope = pl.BlockSpec((2, TM, LANES), lambda b, i: (0, i, 0))
    sds = jax.ShapeDtypeStruct

    def stream_specs(off):
        if len(streams) == 1:
            return [pl.BlockSpec((1, TM, d), lambda b, i: (b, i + off, 0))]
        return [pl.BlockSpec((1, TM, d), lambda b, i: (b, 0, 0)),
                pl.BlockSpec((1, TM, d), lambda b, i: (b, jnp.maximum(i - 1, 0), 0))]

    qa, ka, va, qb, kb, vb, sg = pl.pallas_call(
        functools.partial(_att_proj_kernel, n_streams=len(streams)),
        grid=(bsz, nt),
        in_specs=stream_specs(0) + [_mod_spec(), full((d, ATT_COLS)), full((MLA_HEADS * LANES, MLA_Q_RANK)),
                  full((MLA_KV_RANK, MLA_HEADS * LANES)), full((MLA_WIDTH, MLA_KV_RANK)),
                  full((1, MLA_Q_RANK)), full((1, MLA_KV_RANK)), full((1, GQA_WIDTH)), full((1, LANES)),
                  rope, pl.BlockSpec((2, LANES, TM), lambda b, i: (0, 0, i)), rope],
        out_specs=[head_t(MLA_HEADS), head(MLA_HEADS), head_t(MLA_HEADS // 2), head_t(GQA_HEADS),
                   head(GQA_KV_HEADS), head_t(GQA_KV_HEADS), tok(2 * 512)],
        out_shape=[sds((bsz, MLA_HEADS, LANES, n_all), BF16), sds((bsz, MLA_HEADS, n_all, LANES), BF16),
                   sds((bsz, MLA_HEADS // 2, LANES, n_all), BF16), sds((bsz, GQA_HEADS, LANES, n_all), BF16),
                   sds((bsz, GQA_KV_HEADS, n_all, LANES), BF16), sds((bsz, GQA_KV_HEADS, LANES, n_all), BF16),
                   sds((bsz, n_all, 2 * 512), BF16)],
        compiler_params=_params("parallel", "parallel"),
        name="att_proj",
    )(*streams, msel, win, wq, wk, wv, q_norm_a[None, :], kv_norm_a[None, :],
      jnp.tile(q_norm_b, GQA_HEADS)[None, :], jnp.tile(k_norm_b, 2)[None, :],
      rope_a, jnp.swapaxes(rope_a, 1, 2), rope_b)

    whole = lambda n: pl.BlockSpec((1, n, n_all, LANES), lambda b, i: (b, 0, 0, 0))
    whole_t = lambda n: pl.BlockSpec((1, n, LANES, n_all), lambda b, i: (b, 0, 0, 0))
    mix_a = pl.pallas_call(
        functools.partial(_attn_kernel, n_all=n_all, n_heads=MLA_HEADS, dv=MLA_V, k_of=lambda h: h,
                          v_of=lambda h: (h // 2, (h % 2) * MLA_V)),
        grid=(bsz, nt),
        in_specs=[head_t(MLA_HEADS), whole(MLA_HEADS), whole_t(MLA_HEADS // 2), tok(MLA_WIDTH)],
        out_specs=tok(MLA_WIDTH),
        out_shape=sds((bsz, n_all, MLA_WIDTH), BF16),
        scratch_shapes=[pltpu.VMEM((n_all, TM), F32)] * 2,
        compiler_params=_params("parallel", "parallel"),
        name="mla_attention",
    )(qa, ka, va, sg)
    group = GQA_HEADS // GQA_KV_HEADS
    mix_b = pl.pallas_call(
        functools.partial(_attn_kernel, n_all=n_all, n_heads=GQA_HEADS, dv=GQA_HEAD_DIM, k_of=lambda h: h // group,
                          v_of=lambda h: (h // group, 0)),
        grid=(bsz, nt),
        in_specs=[head_t(GQA_HEADS), whole(GQA_KV_HEADS), whole_t(GQA_KV_HEADS),
                  pl.BlockSpec((1, TM, GQA_WIDTH), lambda b, i: (b, i, 1))],
        out_specs=tok(GQA_WIDTH),
        out_shape=sds((bsz, n_all, GQA_WIDTH), BF16),
        scratch_shapes=[pltpu.VMEM((n_all, TM), F32)] * 2,
        compiler_params=_params("parallel", "parallel"),
        name="gqa_attention",
    )(qb, kb, vb, sg)

    off = 1 if last else 0
    tok_in = lambda w: pl.BlockSpec((1, TM, w), lambda b, i: (b, i + off, 0))
    assert not (last and len(streams) > 1)
    return pl.pallas_call(
        functools.partial(_att_out_kernel, n_streams=len(streams)),
        grid=(bsz, nt - off),
        in_specs=stream_specs(off) + [tok_in(MLA_WIDTH), tok_in(GQA_WIDTH), full((MLA_WIDTH + GQA_WIDTH, d)),
                                      _mod_spec(off), full((1, d)), full((1, d))],
        out_specs=tok(d),
        out_shape=sds((bsz, n_all - off * TM, d), F32),
        compiler_params=_params("parallel", "parallel"),
        name="att_out",
    )(*streams, mix_a, mix_b, w_out.astype(BF16), msel, ln_g[None, :], ln_b[None, :])


def _rec_proj_kernel(x_ref, xp_ref, xn_ref, m_ref, win_ref, cw_ref, cb_ref, gp_ref,
                     q_ref, k_ref, v_ref, bg_ref, sg_ref, xl_ref, *, nt):
    i = pl.program_id(1)
    shift = m_ref[0, 0, 0:1, :]
    scale = m_ref[0, 0, 1:2, :]
    mod = lambda x: x * (1.0 + scale) + shift
    prev_ok = (i >= 2).astype(F32)
    next_ok = jnp.logical_and(i >= 1, i < nt - 1).astype(F32)
    h = mod(x_ref[0])
    hext = jnp.concatenate([mod(xp_ref[0]) * prev_ok, h, mod(xn_ref[0]) * next_ok], 0)
    pext = _bdot(hext, win_ref[:, 0:REC_CONV_COLS])
    rest = _bdot(h, win_ref[:, REC_CONV_COLS:])
    n_ext = TM + 2 * SUBLANES
    y = cw_ref[1:2, :] * pext[SUBLANES:SUBLANES + TM]
    for j in (0, 2, 3):
        y = y + cw_ref[j:j + 1, :] * pltpu.roll(pext, (1 - j) % n_ext, 0)[SUBLANES:SUBLANES + TM]
    qkv = _silu(y[:, 0:3 * GDN_WIDTH])
    xl_ref[0] = y[:, 3 * GDN_WIDTH:] + cb_ref[...]
    for hd in range(GDN_HEADS):
        sl = slice(hd * LANES, (hd + 1) * LANES)
        q = qkv[:, sl]
        k = qkv[:, GDN_WIDTH + hd * LANES:GDN_WIDTH + (hd + 1) * LANES]
        q_ref[0, :, sl] = q * (lax.rsqrt(jnp.sum(q * q, -1, keepdims=True) + EPS) * GDN_HEAD_DIM ** -0.5)
        k_ref[0, :, sl] = k * lax.rsqrt(jnp.sum(k * k, -1, keepdims=True) + EPS)
    v_ref[0] = qkv[:, 2 * GDN_WIDTH:]
    sg_ref[0] = _silu(rest[:, 0:2 * 512]).astype(BF16)
    ba = rest[:, 2 * 512:]
    t = ba + gp_ref[1:2, :]
    softplus = jnp.maximum(t, 0.0) + jnp.log(1.0 + jnp.exp(-jnp.abs(t)))
    decay = -jnp.exp(gp_ref[0:1, :]) * softplus
    bg_ref[0] = jnp.where(_lane_iota(ba.shape) < 2 * GDN_HEADS, jax.nn.sigmoid(ba), decay)


def _each(f, *lists):
    return [f(*args) for args in zip(*lists)]


def _dots(xs, ys):
    return _each(_bdot, xs, ys)


def _stacked_dots(tops, bots, ys):
    n = tops[0].shape[0]
    both = [_bdot(jnp.concatenate([t, b], 0), y) for t, b, y in zip(tops, bots, ys)]
    return [o[:n] for o in both], [o[n:] for o in both]


def _tri_inverse(ms, strict_blocks, eye):
    mds = [jnp.where(strict_blocks, 0.0, m) for m in ms]
    ls = [jnp.where(strict_blocks, m, 0.0) for m in ms]
    add = lambda xs, ys: _each(lambda x, y: x + y, xs, ys)
    ps = [eye - md for md in mds]
    pws = _dots(mds, mds)
    for _ in range(2):
        pws, pn = _stacked_dots(pws, ps, pws)
        ps = add(ps, pn)
    dinvs = add(ps, _dots(ps, pws))
    es = _dots(dinvs, ls)
    qs = [eye - e for e in es]
    pws = _dots(es, es)
    pws, qn = _stacked_dots(pws, qs, pws)
    qs = add(qs, qn)
    qs = add(qs, _dots(qs, pws))
    return _dots(qs, dinvs)


def _gdn_kernel(*refs):
    c = GDN_CHUNK
    ins, outs = (refs[0:4], refs[4:8]), refs[8:10]
    s_ref, u_ref, ops_ref, dec_ref = refs[10:14]

    @pl.when(pl.program_id(1) == 0)
    def _():
        for ref in (s_ref, u_ref, ops_ref, dec_ref):
            ref[...] = jnp.zeros_like(ref)

    streams = [(d, hd) for d in range(2) for hd in range(GDN_HEADS)]
    lanes = lambda hd: slice(hd * LANES, (hd + 1) * LANES)
    ss = [s_ref[d, hd] for d, hd in streams]
    on_state = _dots([ops_ref[d, hd, 0] for d, hd in streams], ss)
    v_news = _each(lambda u, ws: u - ws[:c], [u_ref[d, hd] for d, hd in streams], on_state)
    on_new = _dots([ops_ref[d, hd, 1] for d, hd in streams], v_news)
    for (d, hd), s, a, b in zip(streams, ss, on_state, on_new):
        s_ref[d, hd] = s * dec_ref[d, hd, 0:1, :] + b[c:]
        outs[d][0, :, lanes(hd)] = a[c:] + b[:c]

    row_i = lax.broadcasted_iota(jnp.int32, (c, c), 0)
    col_i = lax.broadcasted_iota(jnp.int32, (c, c), 1)
    eye = (row_i == col_i).astype(F32)
    strict_blocks = (row_i // GDN_DIAG) != (col_i // GDN_DIAG)
    incls = [row_i >= col_i, row_i <= col_i]
    stricts = [row_i > col_i, row_i < col_i]
    lasts = [c - 1, 0]
    bgs = [ins[d][3][0] for d in range(2)]

    qs = [ins[d][0][0, :, lanes(hd)] for d, hd in streams]
    ks = [ins[d][1][0, :, lanes(hd)] for d, hd in streams]
    vs = [ins[d][2][0, :, lanes(hd)] for d, hd in streams]
    betas = [bgs[d][:, d * GDN_HEADS + hd:d * GDN_HEADS + hd + 1] for d, hd in streams]
    gs = [jnp.broadcast_to(bgs[d][:, (2 + d) * GDN_HEADS + hd:(2 + d) * GDN_HEADS + hd + 1], (c, c))
          for d, hd in streams]
    gcums = []
    for d in range(2):
        parts = [_split_bf16(g, 3) for g, (sd, _) in zip(gs, streams) if sd == d]
        wide = jnp.dot(incls[d].astype(BF16), jnp.concatenate([p for ps in parts for p in ps], 1),
                       preferred_element_type=F32)
        gcums += [wide[:, (3 * i) * c:(3 * i + 1) * c] + wide[:, (3 * i + 1) * c:(3 * i + 2) * c]
                  + wide[:, (3 * i + 2) * c:(3 * i + 3) * c] for i in range(GDN_HEADS)]
    gcum_ts = [g.T for g in gcums]
    g_lasts = [g[lasts[d]:lasts[d] + 1, :] for g, (d, _) in zip(gcums, streams)]
    decays = [jnp.exp(jnp.where(incls[d], g - gt, -1e30)) for g, gt, (d, _) in zip(gcums, gcum_ts, streams)]
    es = [jnp.exp(g) for g in gcums]
    kts = [k.T for k in ks]
    kbs = _each(lambda k, b: k * b, ks, betas)
    kks, qks = _stacked_dots(kbs, qs, kts)
    ms = [jnp.where(stricts[d], kk * dec, 0.0) for kk, dec, (d, _) in zip(kks, decays, streams)]
    attns = _each(lambda a, dec: a * dec, qks, decays)
    ts = _tri_inverse(ms, strict_blocks, eye)
    sols = _each(lambda t, v, b, kb, e: _bdot(t, jnp.concatenate([v * b, kb * e], 1)), ts, vs, betas, kbs, es)
    for (d, hd), sol, attn, q, e, kt, gl, gt in zip(streams, sols, attns, qs, es, kts, g_lasts, gcum_ts):
        u_ref[d, hd] = sol[:, :LANES]
        ops_ref[d, hd, 0, 0:c] = sol[:, LANES:].astype(BF16)
        ops_ref[d, hd, 0, c:2 * c] = (q * e).astype(BF16)
        ops_ref[d, hd, 1, 0:c] = attn.astype(BF16)
        ops_ref[d, hd, 1, c:2 * c] = (kt * jnp.exp(gl - gt)).astype(BF16)
        dec_ref[d, hd] = jnp.broadcast_to(jnp.exp(gl), (SUBLANES, c))


def _lru_kernel(xl_ref, wg_ref, bgate_ref, lam_ref, o_ref, a_ref, b_ref, *, nt):
    w = xl_ref.shape[-1]
    rows = lax.broadcasted_iota(jnp.int32, (SUBLANES, w), 0)
    groups = TM // SUBLANES
    for d in range(2):
        rev = d == 1
        lam = lam_ref[d:d + 1, :]
        log_sig = jnp.minimum(lam, 0.0) - jnp.log(1.0 + jnp.exp(-jnp.abs(lam)))

        def tile_body(s, carry):
            if rev:
                t = jnp.where(s == 0, 0, nt - s)
            else:
                t = s
            base = pl.multiple_of(t * TM, TM)
            x = xl_ref[0, pl.ds(base, TM), :]
            xb = x.astype(BF16)
            sigmoid = lambda t: 0.5 * jnp.tanh(0.5 * t) + 0.5
            r = sigmoid(jnp.dot(xb, wg_ref[d, 0, 0], preferred_element_type=F32) + bgate_ref[d, 0, 0])
            gi = sigmoid(jnp.dot(xb, wg_ref[d, 1, 0], preferred_element_type=F32) + bgate_ref[d, 1, 0])
            a = jnp.exp(LRU_C * r * log_sig)
            a_ref[...] = a
            b_ref[...] = jnp.sqrt(1.0 - a * a) * (gi * x)

            def group_body(gidx, h_prev):
                gq = (groups - 1 - gidx) if rev else gidx
                off = pl.multiple_of(gq * SUBLANES, SUBLANES)
                a = a_ref[pl.ds(off, SUBLANES), :]
                bv = b_ref[pl.ds(off, SUBLANES), :]
                for sft in (1, 2, 4):
                    if rev:
                        ok = rows < SUBLANES - sft
                        a_s = pltpu.roll(a, SUBLANES - sft, 0)
                        b_s = pltpu.roll(bv, SUBLANES - sft, 0)
                    else:
                        ok = rows >= sft
                        a_s = pltpu.roll(a, sft, 0)
                        b_s = pltpu.roll(bv, sft, 0)
                    bv = jnp.where(ok, a * b_s + bv, bv)
                    a = jnp.where(ok, a * a_s, a)
                hcur = a * h_prev + bv
                dst = pl.ds(base + off, SUBLANES)
                if rev:
                    o_ref[0, dst, :] = o_ref[0, dst, :] + hcur
                    return jnp.broadcast_to(hcur[0:1, :], hcur.shape)
                o_ref[0, dst, :] = hcur
                return jnp.broadcast_to(hcur[SUBLANES - 1:SUBLANES, :], hcur.shape)

            return lax.fori_loop(0, groups, group_body, carry, unroll=4)

        lax.fori_loop(0, nt, tile_body, jnp.zeros((SUBLANES, w), F32))


def _rec_out_kernel(of_ref, ob_ref, r_ref, sg_ref, gn_ref, w_ref, x_ref, m_ref, g_ref, b_ref, o_ref):
    o = of_ref[0] + ob_ref[0]
    sg = sg_ref[0].astype(F32)
    y = jnp.dot((r_ref[0] * sg[:, GDN_WIDTH:]).astype(BF16), w_ref[GDN_WIDTH:, :], preferred_element_type=F32)
    for hd in range(GDN_HEADS):
        sl = slice(hd * LANES, (hd + 1) * LANES)
        oh = o[:, sl]
        og = oh * lax.rsqrt(jnp.mean(oh * oh, -1, keepdims=True) + EPS) * gn_ref[...] * sg[:, sl]
        y = y + jnp.dot(og.astype(BF16), w_ref[sl, :], preferred_element_type=F32)
    gate = m_ref[0, 0, 2:3, :]
    o_ref[0] = _layer_norm(DEEPNORM_ALPHA * x_ref[0] + gate * y, g_ref[...], b_ref[...])


def _rec_weights(w_in, gdn_conv_w, lru_conv_w, gdn_a_log, gdn_dt_bias, lru_gate_w):
    d = D_MODEL
    o = np.cumsum((0, 3 * GDN_WIDTH, GDN_WIDTH, 2 * GDN_HEADS, 2 * GDN_HEADS, LRU_WIDTH, LRU_WIDTH))
    qkv, z, b, a, xr, gr = (w_in[:, o[i]:o[i + 1]] for i in range(6))
    pieces = [qkv, xr, z, gr, b, a, jnp.zeros((d, LANES - 4 * GDN_HEADS), F32)]
    win = jnp.concatenate([p.astype(BF16) for p in pieces], 1)
    conv_w = jnp.pad(jnp.concatenate([gdn_conv_w, lru_conv_w], 1), ((0, SUBLANES - 4), (0, 0)))
    pad = lambda p: jnp.pad(p.reshape(-1), (2 * GDN_HEADS, LANES - 4 * GDN_HEADS))
    gparams = jnp.pad(jnp.stack([pad(gdn_a_log), pad(gdn_dt_bias)], 0), ((0, SUBLANES - 2), (0, 0)))
    half = LRU_WIDTH // 2
    blocks = lru_gate_w.reshape(2, 2, 2, LRU_BLOCKS // 2, LRU_BLOCK_W, LRU_BLOCK_W)
    eye = jnp.eye(LRU_BLOCKS // 2, dtype=F32)
    wg = jnp.einsum('dghncm,nk->dghnckm', blocks, eye).reshape(2, 2, 2, half, half)
    return win, conv_w, gparams, wg.astype(BF16)


def _recurrent_layer(xs, msel, w_in, w_out, gdn_conv_w, gdn_a_log, gdn_dt_bias, gdn_norm, lru_conv_w,
                     lru_conv_b, lru_gate_w, lru_gate_b, lru_lambda, ln_g, ln_b, last):
    bsz, n_all, d = xs.shape
    nt = n_all // TM
    win, conv_w, gparams, wg = _rec_weights(w_in, gdn_conv_w, lru_conv_w, gdn_a_log, gdn_dt_bias, lru_gate_w)
    full = lambda shape: pl.BlockSpec(shape, lambda b, i: (0,) * len(shape))
    tok = lambda w: pl.BlockSpec((1, TM, w), lambda b, i: (b, i, 0))
    sds = jax.ShapeDtypeStruct
    per = TM // SUBLANES
    n8 = n_all // SUBLANES
    q, k, v, bg, sg, xl = pl.pallas_call(
        functools.partial(_rec_proj_kernel, nt=nt),
        grid=(bsz, nt),
        in_specs=[tok(d),
                  pl.BlockSpec((1, SUBLANES, d), lambda b, i: (b, jnp.maximum(i * per - 1, 0), 0)),
                  pl.BlockSpec((1, SUBLANES, d), lambda b, i: (b, jnp.minimum((i + 1) * per, n8 - 1), 0)),
                  _mod_spec(), full((d, REC_COLS)), full((SUBLANES, REC_CONV_COLS)), full((1, LRU_WIDTH)),
                  full((SUBLANES, LANES))],
        out_specs=[tok(GDN_WIDTH), tok(GDN_WIDTH), tok(GDN_WIDTH), tok(LANES), tok(2 * 512), tok(LRU_WIDTH)],
        out_shape=[sds((bsz, n_all, GDN_WIDTH), F32)] * 3 + [sds((bsz, n_all, LANES), F32),
                   sds((bsz, n_all, 2 * 512), BF16), sds((bsz, n_all, LRU_WIDTH), F32)],
        compiler_params=_params("parallel", "parallel"),
        name="rec_proj",
    )(xs, xs, xs, msel, win, conv_w, lru_conv_b[None, :], gparams)

    nch = n_all // GDN_CHUNK
    nctx = CTX_LEN // GDN_CHUNK
    fwd = lambda s: s
    bwd = lambda s: jnp.where(s < nctx, nctx - 1 - s, nch - 1 + nctx - s)
    prep = lambda order: (lambda b, s: (b, order(jnp.minimum(s, nch - 1)), 0))
    apply_ = lambda order: (lambda b, s: (b, order(jnp.maximum(s - 1, 0)), 0))
    chunk_specs = lambda cmap: [pl.BlockSpec((1, GDN_CHUNK, GDN_WIDTH), cmap)] * 3 + [pl.BlockSpec((1, GDN_CHUNK, LANES), cmap)]
    per_stream = (2, GDN_HEADS)
    outs = pl.pallas_call(
        _gdn_kernel,
        grid=(bsz, nch + 1),
        in_specs=chunk_specs(prep(fwd)) + chunk_specs(prep(bwd)),
        out_specs=[pl.BlockSpec((1, GDN_CHUNK, GDN_WIDTH), apply_(fwd)),
                   pl.BlockSpec((1, GDN_CHUNK, GDN_WIDTH), apply_(bwd))],
        out_shape=[sds((bsz, n_all, GDN_WIDTH), F32)] * 2,
        scratch_shapes=[pltpu.VMEM(per_stream + (GDN_HEAD_DIM, GDN_HEAD_DIM), F32),
                        pltpu.VMEM(per_stream + (GDN_CHUNK, GDN_HEAD_DIM), F32),
                        pltpu.VMEM(per_stream + (2, 2 * GDN_CHUNK, GDN_CHUNK), BF16),
                        pltpu.VMEM(per_stream + (SUBLANES, GDN_CHUNK), F32)],
        compiler_params=_params("parallel", "arbitrary"),
        name="gdn",
    )(q, k, v, bg, q, k, v, bg)

    half = LRU_WIDTH // 2
    r = pl.pallas_call(
        functools.partial(_lru_kernel, nt=nt),
        grid=(bsz, 2),
        in_specs=[pl.BlockSpec((1, n_all, half), lambda b, c: (b, 0, c)),
                  pl.BlockSpec((2, 2, 1, half, half), lambda b, c: (0, 0, c, 0, 0)),
                  pl.BlockSpec((2, 2, 1, 1, half), lambda b, c: (0, 0, c, 0, 0)),
                  pl.BlockSpec((2, half), lambda b, c: (0, c))],
        out_specs=pl.BlockSpec((1, n_all, half), lambda b, c: (b, 0, c)),
        out_shape=sds((bsz, n_all, LRU_WIDTH), F32),
        scratch_shapes=[pltpu.VMEM((TM, half), F32), pltpu.VMEM((TM, half), F32)],
        compiler_params=_params("parallel", "parallel"),
        name="rg_lru",
    )(xl, wg, lru_gate_b.reshape(2, 2, 2, 1, half), lru_lambda)

    off = 1 if last else 0
    tok_in = lambda w: pl.BlockSpec((1, TM, w), lambda b, i: (b, i + off, 0))
    return pl.pallas_call(
        _rec_out_kernel,
        grid=(bsz, nt - off),
        in_specs=[tok_in(GDN_WIDTH), tok_in(GDN_WIDTH), tok_in(LRU_WIDTH), tok_in(2 * 512), full((1, LANES)),
                  full((GDN_WIDTH + LRU_WIDTH, d)), tok_in(d), _mod_spec(off), full((1, d)), full((1, d))],
        out_specs=tok(d),
        out_shape=sds((bsz, n_all - off * TM, d), F32),
        compiler_params=_params("parallel", "parallel"),
        name="rec_out",
    )(outs[0], outs[1], r, sg, gdn_norm[None, :], w_out.astype(BF16), xs, msel, ln_g[None, :], ln_b[None, :])


def kernel(x, c, ctx, c_ctx, mod_w, mod_b, ln_g, ln_b, att_w_in, att_w_out, mla_q_norm, mla_w_uq, mla_kv_norm, mla_w_ukv, gqa_q_norm, gqa_k_norm, rec_w_in, rec_w_out, gdn_conv_w, gdn_a_log, gdn_dt_bias, gdn_norm, lru_conv_w, lru_conv_b, lru_gate_w, lru_gate_b, lru_lambda):
    bsz, n_seq, _ = x.shape
    assert ctx.shape[1] == CTX_LEN == TM and n_seq % TM == 0 and bsz < SUBLANES
    rope_a, rope_b = _rope_tables(n_seq)
    mods = _modulation(c, c_ctx, mod_w, mod_b)
    xs = (ctx, x)
    for layer in range(DEPTH):
        last = layer == DEPTH - 1
        li = layer // 2
        msel = _mod_select(mods[layer], bsz)
        if layer % 2 == 0:
            xs = _attention_layer(xs, msel, att_w_in[li], att_w_out[li], mla_q_norm[li], mla_w_uq[li],
                                  mla_kv_norm[li], mla_w_ukv[li], gqa_q_norm[li], gqa_k_norm[li],
                                  rope_a, rope_b, ln_g[layer], ln_b[layer], last)
        else:
            xs = _recurrent_layer(xs, msel, rec_w_in[li], rec_w_out[li], gdn_conv_w[li], gdn_a_log[li],
                                  gdn_dt_bias[li], gdn_norm[li], lru_conv_w[li], lru_conv_b[li],
                                  lru_gate_w[li], lru_gate_b[li], lru_lambda[li], ln_g[layer], ln_b[layer], last)
    return xs
```

```python
import functools

import numpy as np
import jax
import jax.numpy as jnp
from jax import lax
from jax.experimental import pallas as pl
from jax.experimental.pallas import tpu as pltpu

F32 = jnp.float32
BF16 = jnp.bfloat16

D_MODEL = 1024
DEPTH = 4
GRID_W = 64
CTX_LEN = 256
ROPE_THETA = 10000.0
EPS = 1e-6

MLA_HEADS = 8
MLA_Q_RANK = 256
MLA_KV_RANK = 128
MLA_NOPE = 64
MLA_ROPE = 32
MLA_V = 64
MLA_WIDTH = MLA_HEADS * MLA_V
GQA_HEADS = 8
GQA_KV_HEADS = 2
GQA_HEAD_DIM = 64
GQA_WIDTH = GQA_HEADS * GQA_HEAD_DIM
GDN_HEADS = 4
GDN_HEAD_DIM = 128
GDN_WIDTH = GDN_HEADS * GDN_HEAD_DIM
LRU_WIDTH = 512
LRU_BLOCKS = 8
LRU_BLOCK_W = LRU_WIDTH // LRU_BLOCKS
LRU_C = 8.0
DEEPNORM_ALPHA = (2 * DEPTH) ** 0.25

LANES = 128
SUBLANES = 8
TM = 256
GDN_CHUNK = 128
GDN_DIAG = 16
KEY_CHUNK = 512
PV_LAG = 4
LOG2_E = 1.4426950408889634
VMEM_LIMIT = 48 * 1024 * 1024

ATT_COLS = 2048
REC_CONV_COLS = 3 * GDN_WIDTH + LRU_WIDTH
REC_COLS = REC_CONV_COLS + 2 * 512 + LANES


def _params(*sem):
    return pltpu.CompilerParams(dimension_semantics=sem, vmem_limit_bytes=VMEM_LIMIT)


def _bdot(a, b):
    return jnp.dot(a.astype(BF16), b.astype(BF16), preferred_element_type=F32)


def _split_bf16(a, parts):
    out = []
    for _ in range(parts):
        hi = a.astype(BF16)
        out.append(hi)
        a = a - hi.astype(F32)
    return out


def _dot_split(a, b):
    a_hi, a_lo = _split_bf16(a, 2)
    b_hi, b_lo = _split_bf16(b, 2)
    d = functools.partial(jnp.dot, preferred_element_type=F32)
    return d(a_hi, b_hi) + (d(a_lo, b_hi) + d(a_hi, b_lo))


def _silu(x):
    return x * jax.nn.sigmoid(x)


def _lane_iota(shape):
    return lax.broadcasted_iota(jnp.int32, shape, len(shape) - 1)


def _swap_groups(x, n):
    fwd = pltpu.roll(x, LANES - n, 1)
    bwd = pltpu.roll(x, n, 1)
    return jnp.where((_lane_iota(x.shape) % (2 * n)) < n, fwd, bwd)


def _rope(x, cos, sin, n):
    return x * cos + _swap_groups(x, n) * sin


def _layer_norm(z, g, b):
    mu = jnp.mean(z, -1, keepdims=True)
    zc = z - mu
    var = jnp.mean(zc * zc, -1, keepdims=True)
    return zc * lax.rsqrt(var + EPS) * g + b


def _mod_kernel(c_ref, w_ref, b_ref, o_ref):
    c = c_ref[...]
    o_ref[0] = _dot_split(_silu(c), w_ref[0]) + b_ref[0]


def _modulation(c, c_ctx, mod_w, mod_b):
    bsz = c.shape[0]
    rows = jnp.concatenate([c, c_ctx[None, :], jnp.zeros((SUBLANES - bsz - 1, D_MODEL), F32)], 0)
    tn = 1024
    return pl.pallas_call(
        _mod_kernel,
        grid=(DEPTH, 3 * D_MODEL // tn),
        in_specs=[pl.BlockSpec((SUBLANES, D_MODEL), lambda l, n: (0, 0)),
                  pl.BlockSpec((1, D_MODEL, tn), lambda l, n: (l, 0, n)),
                  pl.BlockSpec((1, 1, tn), lambda l, n: (l, 0, n))],
        out_specs=pl.BlockSpec((1, SUBLANES, tn), lambda l, n: (l, 0, n)),
        out_shape=jax.ShapeDtypeStruct((DEPTH, SUBLANES, 3 * D_MODEL), F32),
        compiler_params=_params("parallel", "parallel"),
        name="modulation",
    )(rows, mod_w, mod_b.reshape(DEPTH, 1, 3 * D_MODEL))


def _mod_select(mods_l, bsz):
    m3 = mods_l.reshape(SUBLANES, 3, D_MODEL)
    ctx_m = jnp.broadcast_to(m3[bsz], (bsz, 3, D_MODEL))
    sel = jnp.stack([ctx_m, m3[:bsz]], 1)
    return jnp.pad(sel, ((0, 0), (0, 0), (0, SUBLANES - 3), (0, 0)))


def _mod_spec(off=0):
    return pl.BlockSpec((1, 1, SUBLANES, D_MODEL), lambda b, i: (b, jnp.minimum(i + off, 1), 0, 0))


def _token_tile(refs, n_streams):
    if n_streams == 1:
        return refs[0][0]
    return jnp.where(pl.program_id(1) == 0, refs[0][0], refs[1][0])


def _att_proj_kernel(*refs, n_streams):
    (m_ref, win_ref, wuq_ref, wk_ref, wv_ref, wqb_ref, nq_ref, nkv_ref, nqb_ref, nkb_ref, ra_ref, rat_ref, rb_ref,
     rbt_ref, qa_ref, ka_ref, va_ref, qb_ref, kb_ref, vb_ref, sg_ref) = refs[n_streams:]
    shift = m_ref[0, 0, 0:1, :]
    scale = m_ref[0, 0, 1:2, :]
    h = _token_tile(refs, n_streams) * (1.0 + scale) + shift
    p = _bdot(h, win_ref[...])
    cq, ckv, kr = p[:, 0:256], p[:, 256:384], p[:, 384:512]
    gates = p[:, 512:1536]
    kb, vb = p[:, 1536:1792], p[:, 1792:2048]
    sg_ref[0] = _silu(gates).astype(BF16)

    def rms(x):
        return x * lax.rsqrt(jnp.mean(x * x, -1, keepdims=True) + EPS)

    qa_t = _bdot(wuq_ref[...], (rms(cq) * nq_ref[...]).T)
    ckvn = rms(ckv) * nkv_ref[...]
    kn = _bdot(ckvn, wk_ref[...])
    va_t = _bdot(wv_ref[...], ckvn.T)
    for j in range(MLA_HEADS // 2):
        va_ref[0, j] = va_t[j * LANES:(j + 1) * LANES].astype(BF16)
    cos_a, sin_a = ra_ref[0], ra_ref[1]
    kr = _rope(kr, cos_a, sin_a, MLA_ROPE // 4)
    cos_t, sin_t = rat_ref[0], rat_ref[1]
    q_scale = LOG2_E * (MLA_NOPE + MLA_ROPE) ** -0.5
    r0, n8 = MLA_NOPE, MLA_ROPE // 4
    for hd in range(MLA_HEADS):
        sl = slice(hd * LANES, (hd + 1) * LANES)
        qt = qa_t[sl]
        swapped = jnp.concatenate([qt[:r0], qt[r0 + n8:r0 + 2 * n8], qt[r0:r0 + n8], qt[r0 + 3 * n8:r0 + 4 * n8],
                                   qt[r0 + 2 * n8:r0 + 3 * n8], qt[r0 + 4 * n8:]], 0)
        qa_ref[0, hd] = ((qt * cos_t + swapped * sin_t) * q_scale).astype(BF16)
        ka_ref[0, hd] = (kn[:, sl] + kr).astype(BF16)

    cos_b, sin_b = rb_ref[0], rb_ref[1]
    qb_t = _bdot(wqb_ref[...], h.T)
    cos_bt, sin_bt = rbt_ref[0, 0:GQA_HEAD_DIM, :], rbt_ref[1, 0:GQA_HEAD_DIM, :]
    n16 = GQA_HEAD_DIM // 4
    zeros = jnp.zeros((GQA_HEAD_DIM, TM), F32)
    for hd in range(GQA_HEADS):
        xq = qb_t[hd * GQA_HEAD_DIM:(hd + 1) * GQA_HEAD_DIM]
        xn = xq * lax.rsqrt(jnp.mean(xq * xq, 0, keepdims=True) + EPS) * nqb_ref[...]
        swapped = jnp.concatenate([xn[n16:2 * n16], xn[0:n16], xn[3 * n16:], xn[2 * n16:3 * n16]], 0)
        qt = (xn * cos_bt + swapped * sin_bt) * (LOG2_E * GQA_HEAD_DIM ** -0.5)
        qb_ref[0, hd] = jnp.concatenate([qt, zeros] if hd % 2 == 0 else [zeros, qt], 0).astype(BF16)
    for g in range(GQA_KV_HEADS):
        sl = slice(g * LANES, (g + 1) * LANES)
        kb_ref[0, g] = _rope(rms(kb[:, sl]) * nkb_ref[...], cos_b, sin_b, GQA_HEAD_DIM // 4).astype(BF16)
        vb_ref[0, g] = vb[:, sl].T.astype(BF16)


def _fold_rows(op, x, acc):
    for r in range(x.shape[0] // SUBLANES):
        acc = op(acc, x[r * SUBLANES:(r + 1) * SUBLANES])
    return acc


def _attend_heads(qtf, kf, vtf, s_refs, n_heads, nk):
    n_chunks = max(nk // KEY_CHUNK, 1)
    bounds = [i * KEY_CHUNK for i in range(n_chunks)] + [nk]
    chunks = [(bounds[i], bounds[i + 1] - bounds[i]) for i in range(n_chunks)]

    def score(h, c0, n, mt):
        s = jnp.dot(kf(h, c0, n), qtf(h), preferred_element_type=F32)
        s_refs[h % 2][c0:c0 + n, :] = s
        return _fold_rows(jnp.maximum, s, mt)

    neg = jnp.full((SUBLANES, TM), -jnp.inf, F32)
    zero = jnp.zeros((SUBLANES, TM), F32)
    acc = [None] * n_heads
    col_sum = [zero] * n_heads
    pending = []

    def pv_oldest():
        h, p, c0, n = pending.pop(0)
        o = jnp.dot(vtf(h, c0, n), p, preferred_element_type=F32)
        acc[h] = o if acc[h] is None else acc[h] + o

    mt = neg
    for c0, n in chunks:
        mt = score(0, c0, n, mt)
    for h in range(n_heads):
        m = jnp.max(mt, 0, keepdims=True)
        mt = neg
        for c0, n in chunks:
            p = jnp.exp2(s_refs[h % 2][c0:c0 + n, :] - m)
            col_sum[h] = _fold_rows(jnp.add, p, col_sum[h])
            pending.append((h, p.astype(BF16), c0, n))
            if h + 1 < n_heads:
                mt = score(h + 1, c0, n, mt)
            if len(pending) > PV_LAG:
                pv_oldest()
    while pending:
        pv_oldest()
    return [acc[h] / jnp.sum(col_sum[h], 0, keepdims=True) for h in range(n_heads)]


def _store_head_pairs(outs, sg_ref, o_ref):
    for j in range(len(outs) // 2):
        sl = slice(j * LANES, (j + 1) * LANES)
        pair = jnp.concatenate([outs[2 * j], outs[2 * j + 1]], 0).T
        o_ref[0, :, sl] = (pair * sg_ref[0, :, sl].astype(F32)).astype(BF16)


def _attn_kernel(qt_ref, k_ref, vt_ref, sg_ref, o_ref, sa_ref, sb_ref, *, n_all, n_heads, dv, k_of, v_of):
    def vt_chunk(h, c0, n):
        blk, r0 = v_of(h)
        return vt_ref[0, blk, r0:r0 + dv, c0:c0 + n]

    def run(nk):
        outs = _attend_heads(lambda h: qt_ref[0, h], lambda h, c0, n: k_ref[0, k_of(h), c0:c0 + n, :],
                             vt_chunk, (sa_ref, sb_ref), n_heads, nk)
        _store_head_pairs(outs, sg_ref, o_ref)

    @pl.when(pl.program_id(1) == 0)
    def _():
        run(CTX_LEN)

    @pl.when(pl.program_id(1) > 0)
    def _():
        run(n_all)


def _att_out_kernel(*refs, n_streams):
    ma_ref, mb_ref, w_ref, m_ref, g_ref, b_ref, o_ref = refs[n_streams:]
    y = (jnp.dot(ma_ref[0], w_ref[0:MLA_WIDTH, :], preferred_element_type=F32)
         + jnp.dot(mb_ref[0], w_ref[MLA_WIDTH:, :], preferred_element_type=F32))
    gate = m_ref[0, 0, 2:3, :]
    o_ref[0] = _layer_norm(DEEPNORM_ALPHA * _token_tile(refs, n_streams) + gate * y, g_ref[...], b_ref[...])


def _rope_tables(n_seq):
    pos = np.arange(n_seq)
    row = (pos // GRID_W).astype(np.float32)[:, None]
    col = (pos % GRID_W).astype(np.float32)[:, None]

    def quarter(dim):
        half = dim // 2
        inv = (ROPE_THETA ** (-np.arange(0, half, 2, dtype=np.float32) / half)).astype(np.float32)
        return np.cos(row * inv), np.sin(row * inv), np.cos(col * inv), np.sin(col * inv)

    def with_ctx(cos, sin):
        cos = np.concatenate([np.ones((CTX_LEN, LANES), np.float32), cos], 0)
        sin = np.concatenate([np.zeros((CTX_LEN, LANES), np.float32), sin], 0)
        return jnp.asarray(np.stack([cos, sin], 0).astype(np.float32))

    cr, sr, cc, sc = quarter(MLA_ROPE)
    ones = lambda n: np.ones((n_seq, n), np.float32)
    cos_a = np.concatenate([ones(MLA_NOPE), cr, cr, cc, cc, ones(32)], 1)
    sin_a = np.concatenate([0 * ones(MLA_NOPE), -sr, sr, -sc, sc, 0 * ones(32)], 1)
    cr, sr, cc, sc = quarter(GQA_HEAD_DIM)
    cos_b = np.concatenate([cr, cr, cc, cc] * 2, 1)
    sin_b = np.concatenate([-sr, sr, -sc, sc] * 2, 1)
    return with_ctx(cos_a, sin_a), with_ctx(cos_b, sin_b)


def _att_weights(w_in, w_uq, w_ukv):
    d = D_MODEL
    o = np.cumsum((0, MLA_Q_RANK, MLA_KV_RANK, MLA_ROPE, MLA_WIDTH, GQA_WIDTH, 128, 128, GQA_WIDTH))
    cq, ckv, kr, ga, qb, kb, vb, gb = (w_in[:, o[i]:o[i + 1]] for i in range(8))
    z = lambda n: jnp.zeros((d, n), F32)
    dup = lambda w: [w[:, 0:64], w[:, 0:64], w[:, 64:128], w[:, 64:128]]
    pieces = [cq, ckv, z(MLA_NOPE), kr, z(32), ga, gb] + dup(kb) + dup(vb)
    win = jnp.concatenate([p.astype(BF16) for p in pieces], 1)
    wq = w_uq.reshape(MLA_Q_RANK, MLA_HEADS, MLA_NOPE + MLA_ROPE)
    wq = jnp.pad(wq, ((0, 0), (0, 0), (0, 32))).reshape(MLA_Q_RANK, MLA_HEADS * LANES)
    wkv = w_ukv.reshape(MLA_KV_RANK, MLA_HEADS, MLA_NOPE + MLA_V)
    wk = jnp.pad(wkv[:, :, :MLA_NOPE], ((0, 0), (0, 0), (0, 64))).reshape(MLA_KV_RANK, MLA_HEADS * LANES)
    wv = wkv[:, :, MLA_NOPE:].reshape(MLA_KV_RANK, MLA_WIDTH)
    return win, wq.T.astype(BF16), wk.astype(BF16), wv.T.astype(BF16), qb.T.astype(BF16)


def _attention_layer(xs, msel, w_in, w_out, q_norm_a, w_uq, kv_norm_a, w_ukv, q_norm_b, k_norm_b,
                     rope_a, rope_b, ln_g, ln_b, last):
    streams = xs if isinstance(xs, tuple) else (xs,)
    bsz, d = streams[-1].shape[0], streams[-1].shape[2]
    n_all = sum(a.shape[1] for a in streams)
    nt = n_all // TM
    win, wq, wk, wv, wqb = _att_weights(w_in, w_uq, w_ukv)
    full = lambda shape: pl.BlockSpec(shape, lambda b, i: (0,) * len(shape))
    tok = lambda w: pl.BlockSpec((1, TM, w), lambda b, i: (b, i, 0))
    head = lambda n: pl.BlockSpec((1, n, TM, LANES), lambda b, i: (b, 0, i, 0))
    head_t = lambda n: pl.BlockSpec((1, n, LANES, TM), lambda b, i: (b, 0, 0, i))
    rope = pl.BlockSpec((2, TM, LANES), lambda b, i: (0, i, 0))
    rope_t = pl.BlockSpec((2, LANES, TM), lambda b, i: (0, 0, i))
    sds = jax.ShapeDtypeStruct

    def stream_specs(off):
        if len(streams) == 1:
            return [pl.BlockSpec((1, TM, d), lambda b, i: (b, i + off, 0))]
        return [pl.BlockSpec((1, TM, d), lambda b, i: (b, 0, 0)),
                pl.BlockSpec((1, TM, d), lambda b, i: (b, jnp.maximum(i - 1, 0), 0))]

    qa, ka, va, qb, kb, vb, sg = pl.pallas_call(
        functools.partial(_att_proj_kernel, n_streams=len(streams)),
        grid=(bsz, nt),
        in_specs=stream_specs(0) + [_mod_spec(), full((d, ATT_COLS)), full((MLA_HEADS * LANES, MLA_Q_RANK)),
                  full((MLA_KV_RANK, MLA_HEADS * LANES)), full((MLA_WIDTH, MLA_KV_RANK)), full((GQA_WIDTH, d)),
                  full((1, MLA_Q_RANK)), full((1, MLA_KV_RANK)), full((GQA_HEAD_DIM, 1)), full((1, LANES)),
                  rope, rope_t, rope, rope_t],
        out_specs=[head_t(MLA_HEADS), head(MLA_HEADS), head_t(MLA_HEADS // 2), head_t(GQA_HEADS),
                   head(GQA_KV_HEADS), head_t(GQA_KV_HEADS), tok(2 * 512)],
        out_shape=[sds((bsz, MLA_HEADS, LANES, n_all), BF16), sds((bsz, MLA_HEADS, n_all, LANES), BF16),
                   sds((bsz, MLA_HEADS // 2, LANES, n_all), BF16), sds((bsz, GQA_HEADS, LANES, n_all), BF16),
                   sds((bsz, GQA_KV_HEADS, n_all, LANES), BF16), sds((bsz, GQA_KV_HEADS, LANES, n_all), BF16),
                   sds((bsz, n_all, 2 * 512), BF16)],
        compiler_params=_params("parallel", "parallel"),
        name="att_proj",
    )(*streams, msel, win, wq, wk, wv, wqb, q_norm_a[None, :], kv_norm_a[None, :],
      q_norm_b[:, None], jnp.tile(k_norm_b, 2)[None, :],
      rope_a, jnp.swapaxes(rope_a, 1, 2), rope_b, jnp.swapaxes(rope_b, 1, 2))

    whole = lambda n: pl.BlockSpec((1, n, n_all, LANES), lambda b, i: (b, 0, 0, 0))
    whole_t = lambda n: pl.BlockSpec((1, n, LANES, n_all), lambda b, i: (b, 0, 0, 0))
    mix_a = pl.pallas_call(
        functools.partial(_attn_kernel, n_all=n_all, n_heads=MLA_HEADS, dv=MLA_V, k_of=lambda h: h,
                          v_of=lambda h: (h // 2, (h % 2) * MLA_V)),
        grid=(bsz, nt),
        in_specs=[head_t(MLA_HEADS), whole(MLA_HEADS), whole_t(MLA_HEADS // 2), tok(MLA_WIDTH)],
        out_specs=tok(MLA_WIDTH),
        out_shape=sds((bsz, n_all, MLA_WIDTH), BF16),
        scratch_shapes=[pltpu.VMEM((n_all, TM), F32)] * 2,
        compiler_params=_params("parallel", "parallel"),
        name="mla_attention",
    )(qa, ka, va, sg)
    group = GQA_HEADS // GQA_KV_HEADS
    mix_b = pl.pallas_call(
        functools.partial(_attn_kernel, n_all=n_all, n_heads=GQA_HEADS, dv=GQA_HEAD_DIM, k_of=lambda h: h // group,
                          v_of=lambda h: (h // group, 0)),
        grid=(bsz, nt),
        in_specs=[head_t(GQA_HEADS), whole(GQA_KV_HEADS), whole_t(GQA_KV_HEADS),
                  pl.BlockSpec((1, TM, GQA_WIDTH), lambda b, i: (b, i, 1))],
        out_specs=tok(GQA_WIDTH),
        out_shape=sds((bsz, n_all, GQA_WIDTH), BF16),
        scratch_shapes=[pltpu.VMEM((n_all, TM), F32)] * 2,
        compiler_params=_params("parallel", "parallel"),
        name="gqa_attention",
    )(qb, kb, vb, sg)

    off = 1 if last else 0
    tok_in = lambda w: pl.BlockSpec((1, TM, w), lambda b, i: (b, i + off, 0))
    assert not (last and len(streams) > 1)
    return pl.pallas_call(
        functools.partial(_att_out_kernel, n_streams=len(streams)),
        grid=(bsz, nt - off),
        in_specs=stream_specs(off) + [tok_in(MLA_WIDTH), tok_in(GQA_WIDTH), full((MLA_WIDTH + GQA_WIDTH, d)),
                                      _mod_spec(off), full((1, d)), full((1, d))],
        out_specs=tok(d),
        out_shape=sds((bsz, n_all - off * TM, d), F32),
        compiler_params=_params("parallel", "parallel"),
        name="att_out",
    )(*streams, mix_a, mix_b, w_out.astype(BF16), msel, ln_g[None, :], ln_b[None, :])


def _rec_proj_kernel(x_ref, xp_ref, xn_ref, m_ref, win_ref, cw_ref, cb_ref, gp_ref,
                     q_ref, k_ref, v_ref, bg_ref, sg_ref, xl_ref, *, nt):
    i = pl.program_id(1)
    shift = m_ref[0, 0, 0:1, :]
    scale = m_ref[0, 0, 1:2, :]
    mod = lambda x: x * (1.0 + scale) + shift
    prev_ok = (i >= 2).astype(F32)
    next_ok = jnp.logical_and(i >= 1, i < nt - 1).astype(F32)
    h = mod(x_ref[0])
    hext = jnp.concatenate([mod(xp_ref[0]) * prev_ok, h, mod(xn_ref[0]) * next_ok], 0)
    pext = _bdot(hext, win_ref[:, 0:REC_CONV_COLS])
    rest = _bdot(h, win_ref[:, REC_CONV_COLS:])
    n_ext = TM + 2 * SUBLANES
    y = cw_ref[1:2, :] * pext[SUBLANES:SUBLANES + TM]
    for j in (0, 2, 3):
        y = y + cw_ref[j:j + 1, :] * pltpu.roll(pext, (1 - j) % n_ext, 0)[SUBLANES:SUBLANES + TM]
    qkv = _silu(y[:, 0:3 * GDN_WIDTH])
    xl_ref[0] = y[:, 3 * GDN_WIDTH:] + cb_ref[...]
    for hd in range(GDN_HEADS):
        sl = slice(hd * LANES, (hd + 1) * LANES)
        q = qkv[:, sl]
        k = qkv[:, GDN_WIDTH + hd * LANES:GDN_WIDTH + (hd + 1) * LANES]
        q_ref[0, :, sl] = q * (lax.rsqrt(jnp.sum(q * q, -1, keepdims=True) + EPS) * GDN_HEAD_DIM ** -0.5)
        k_ref[0, :, sl] = k * lax.rsqrt(jnp.sum(k * k, -1, keepdims=True) + EPS)
    v_ref[0] = qkv[:, 2 * GDN_WIDTH:]
    sg_ref[0] = _silu(rest[:, 0:2 * 512]).astype(BF16)
    ba = rest[:, 2 * 512:]
    t = ba + gp_ref[1:2, :]
    softplus = jnp.maximum(t, 0.0) + jnp.log(1.0 + jnp.exp(-jnp.abs(t)))
    decay = -jnp.exp(gp_ref[0:1, :]) * softplus
    bg_ref[0] = jnp.where(_lane_iota(ba.shape) < 2 * GDN_HEADS, jax.nn.sigmoid(ba), decay)


def _each(f, *lists):
    return [f(*args) for args in zip(*lists)]


def _dots(xs, ys):
    return _each(_bdot, xs, ys)


def _stacked_dots(tops, bots, ys):
    n = tops[0].shape[0]
    both = [_bdot(jnp.concatenate([t, b], 0), y) for t, b, y in zip(tops, bots, ys)]
    return [o[:n] for o in both], [o[n:] for o in both]


def _tri_inverse(ms, strict_blocks, eye):
    mds = [jnp.where(strict_blocks, 0.0, m) for m in ms]
    ls = [jnp.where(strict_blocks, m, 0.0) for m in ms]
    add = lambda xs, ys: _each(lambda x, y: x + y, xs, ys)
    ps = [eye - md for md in mds]
    pws = _dots(mds, mds)
    for _ in range(2):
        pws, pn = _stacked_dots(pws, ps, pws)
        ps = add(ps, pn)
    dinvs = add(ps, _dots(ps, pws))
    es = _dots(dinvs, ls)
    qs = [eye - e for e in es]
    pws = _dots(es, es)
    pws, qn = _stacked_dots(pws, qs, pws)
    qs = add(qs, qn)
    qs = add(qs, _dots(qs, pws))
    return _dots(qs, dinvs)


def _gdn_kernel(*refs):
    c = GDN_CHUNK
    ins, outs = (refs[0:4], refs[4:8]), refs[8:10]
    s_ref, u_ref, ops_ref, dec_ref = refs[10:14]

    @pl.when(pl.program_id(1) == 0)
    def _():
        for ref in (s_ref, u_ref, ops_ref, dec_ref):
            ref[...] = jnp.zeros_like(ref)

    streams = [(d, hd) for d in range(2) for hd in range(GDN_HEADS)]
    lanes = lambda hd: slice(hd * LANES, (hd + 1) * LANES)
    ss = [s_ref[d, hd] for d, hd in streams]
    on_state = _dots([ops_ref[d, hd, 0] for d, hd in streams], ss)
    v_news = _each(lambda u, ws: u - ws[:c], [u_ref[d, hd] for d, hd in streams], on_state)
    on_new = _dots([ops_ref[d, hd, 1] for d, hd in streams], v_news)
    for (d, hd), s, a, b in zip(streams, ss, on_state, on_new):
        s_ref[d, hd] = s * dec_ref[d, hd, 0:1, :] + b[c:]
        outs[d][0, :, lanes(hd)] = a[c:] + b[:c]

    row_i = lax.broadcasted_iota(jnp.int32, (c, c), 0)
    col_i = lax.broadcasted_iota(jnp.int32, (c, c), 1)
    eye = (row_i == col_i).astype(F32)
    strict_blocks = (row_i // GDN_DIAG) != (col_i // GDN_DIAG)
    incls = [row_i >= col_i, row_i <= col_i]
    stricts = [row_i > col_i, row_i < col_i]
    lasts = [c - 1, 0]
    bgs = [ins[d][3][0] for d in range(2)]

    qs = [ins[d][0][0, :, lanes(hd)] for d, hd in streams]
    ks = [ins[d][1][0, :, lanes(hd)] for d, hd in streams]
    vs = [ins[d][2][0, :, lanes(hd)] for d, hd in streams]
    betas = [bgs[d][:, d * GDN_HEADS + hd:d * GDN_HEADS + hd + 1] for d, hd in streams]
    gs = [jnp.broadcast_to(bgs[d][:, (2 + d) * GDN_HEADS + hd:(2 + d) * GDN_HEADS + hd + 1], (c, c))
          for d, hd in streams]
    gcums = []
    for d in range(2):
        parts = [_split_bf16(g, 3) for g, (sd, _) in zip(gs, streams) if sd == d]
        wide = jnp.dot(incls[d].astype(BF16), jnp.concatenate([p for ps in parts for p in ps], 1),
                       preferred_element_type=F32)
        gcums += [wide[:, (3 * i) * c:(3 * i + 1) * c] + wide[:, (3 * i + 1) * c:(3 * i + 2) * c]
                  + wide[:, (3 * i + 2) * c:(3 * i + 3) * c] for i in range(GDN_HEADS)]
    gcum_ts = [g.T for g in gcums]
    g_lasts = [g[lasts[d]:lasts[d] + 1, :] for g, (d, _) in zip(gcums, streams)]
    decays = [jnp.exp(jnp.where(incls[d], g - gt, -1e30)) for g, gt, (d, _) in zip(gcums, gcum_ts, streams)]
    es = [jnp.exp(g) for g in gcums]
    kts = [k.T for k in ks]
    kbs = _each(lambda k, b: k * b, ks, betas)
    kks, qks = _stacked_dots(kbs, qs, kts)
    ms = [jnp.where(stricts[d], kk * dec, 0.0) for kk, dec, (d, _) in zip(kks, decays, streams)]
    attns = _each(lambda a, dec: a * dec, qks, decays)
    ts = _tri_inverse(ms, strict_blocks, eye)
    sols = _each(lambda t, v, b, kb, e: _bdot(t, jnp.concatenate([v * b, kb * e], 1)), ts, vs, betas, kbs, es)
    for (d, hd), sol, attn, q, e, kt, gl, gt in zip(streams, sols, attns, qs, es, kts, g_lasts, gcum_ts):
        u_ref[d, hd] = sol[:, :LANES]
        ops_ref[d, hd, 0, 0:c] = sol[:, LANES:].astype(BF16)
        ops_ref[d, hd, 0, c:2 * c] = (q * e).astype(BF16)
        ops_ref[d, hd, 1, 0:c] = attn.astype(BF16)
        ops_ref[d, hd, 1, c:2 * c] = (kt * jnp.exp(gl - gt)).astype(BF16)
        dec_ref[d, hd] = jnp.broadcast_to(jnp.exp(gl), (SUBLANES, c))


def _lru_kernel(xl_ref, wg_ref, bgate_ref, lam_ref, o_ref, a_ref, b_ref, *, nt):
    w = xl_ref.shape[-1]
    rows = lax.broadcasted_iota(jnp.int32, (SUBLANES, w), 0)
    groups = TM // SUBLANES
    for d in range(2):
        rev = d == 1
        lam = lam_ref[d:d + 1, :]
        log_sig = jnp.minimum(lam, 0.0) - jnp.log(1.0 + jnp.exp(-jnp.abs(lam)))

        def tile_body(s, carry):
            if rev:
                t = jnp.where(s == 0, 0, nt - s)
            else:
                t = s
            base = pl.multiple_of(t * TM, TM)
            x = xl_ref[0, pl.ds(base, TM), :]
            xb = x.astype(BF16)
            sigmoid = lambda t: 0.5 * jnp.tanh(0.5 * t) + 0.5
            r = sigmoid(jnp.dot(xb, wg_ref[d, 0, 0], preferred_element_type=F32) + bgate_ref[d, 0, 0])
            gi = sigmoid(jnp.dot(xb, wg_ref[d, 1, 0], preferred_element_type=F32) + bgate_ref[d, 1, 0])
            a = jnp.exp(LRU_C * r * log_sig)
            a_ref[...] = a
            b_ref[...] = jnp.sqrt(1.0 - a * a) * (gi * x)

            def group_body(gidx, h_prev):
                gq = (groups - 1 - gidx) if rev else gidx
                off = pl.multiple_of(gq * SUBLANES, SUBLANES)
                a = a_ref[pl.ds(off, SUBLANES), :]
                bv = b_ref[pl.ds(off, SUBLANES), :]
                for sft in (1, 2, 4):
                    if rev:
                        ok = rows < SUBLANES - sft
                        a_s = pltpu.roll(a, SUBLANES - sft, 0)
                        b_s = pltpu.roll(bv, SUBLANES - sft, 0)
                    else:
                        ok = rows >= sft
                        a_s = pltpu.roll(a, sft, 0)
                        b_s = pltpu.roll(bv, sft, 0)
                    bv = jnp.where(ok, a * b_s + bv, bv)
                    a = jnp.where(ok, a * a_s, a)
                hcur = a * h_prev + bv
                dst = pl.ds(base + off, SUBLANES)
                if rev:
                    o_ref[0, dst, :] = o_ref[0, dst, :] + hcur
                    return jnp.broadcast_to(hcur[0:1, :], hcur.shape)
                o_ref[0, dst, :] = hcur
                return jnp.broadcast_to(hcur[SUBLANES - 1:SUBLANES, :], hcur.shape)

            return lax.fori_loop(0, groups, group_body, carry, unroll=4)

        lax.fori_loop(0, nt, tile_body, jnp.zeros((SUBLANES, w), F32))


def _rec_out_kernel(of_ref, ob_ref, r_ref, sg_ref, gn_ref, w_ref, x_ref, m_ref, g_ref, b_ref, o_ref):
    o = of_ref[0] + ob_ref[0]
    sg = sg_ref[0].astype(F32)
    y = jnp.dot((r_ref[0] * sg[:, GDN_WIDTH:]).astype(BF16), w_ref[GDN_WIDTH:, :], preferred_element_type=F32)
    for hd in range(GDN_HEADS):
        sl = slice(hd * LANES, (hd + 1) * LANES)
        oh = o[:, sl]
        og = oh * lax.rsqrt(jnp.mean(oh * oh, -1, keepdims=True) + EPS) * gn_ref[...] * sg[:, sl]
        y = y + jnp.dot(og.astype(BF16), w_ref[sl, :], preferred_element_type=F32)
    gate = m_ref[0, 0, 2:3, :]
    o_ref[0] = _layer_norm(DEEPNORM_ALPHA * x_ref[0] + gate * y, g_ref[...], b_ref[...])


def _rec_weights(w_in, gdn_conv_w, lru_conv_w, gdn_a_log, gdn_dt_bias, lru_gate_w):
    d = D_MODEL
    o = np.cumsum((0, 3 * GDN_WIDTH, GDN_WIDTH, 2 * GDN_HEADS, 2 * GDN_HEADS, LRU_WIDTH, LRU_WIDTH))
    qkv, z, b, a, xr, gr = (w_in[:, o[i]:o[i + 1]] for i in range(6))
    pieces = [qkv, xr, z, gr, b, a, jnp.zeros((d, LANES - 4 * GDN_HEADS), F32)]
    win = jnp.concatenate([p.astype(BF16) for p in pieces], 1)
    conv_w = jnp.pad(jnp.concatenate([gdn_conv_w, lru_conv_w], 1), ((0, SUBLANES - 4), (0, 0)))
    pad = lambda p: jnp.pad(p.reshape(-1), (2 * GDN_HEADS, LANES - 4 * GDN_HEADS))
    gparams = jnp.pad(jnp.stack([pad(gdn_a_log), pad(gdn_dt_bias)], 0), ((0, SUBLANES - 2), (0, 0)))
    half = LRU_WIDTH // 2
    blocks = lru_gate_w.reshape(2, 2, 2, LRU_BLOCKS // 2, LRU_BLOCK_W, LRU_BLOCK_W)
    eye = jnp.eye(LRU_BLOCKS // 2, dtype=F32)
    wg = jnp.einsum('dghncm,nk->dghnckm', blocks, eye).reshape(2, 2, 2, half, half)
    return win, conv_w, gparams, wg.astype(BF16)


def _recurrent_layer(xs, msel, w_in, w_out, gdn_conv_w, gdn_a_log, gdn_dt_bias, gdn_norm, lru_conv_w,
                     lru_conv_b, lru_gate_w, lru_gate_b, lru_lambda, ln_g, ln_b, last):
    bsz, n_all, d = xs.shape
    nt = n_all // TM
    win, conv_w, gparams, wg = _rec_weights(w_in, gdn_conv_w, lru_conv_w, gdn_a_log, gdn_dt_bias, lru_gate_w)
    full = lambda shape: pl.BlockSpec(shape, lambda b, i: (0,) * len(shape))
    tok = lambda w: pl.BlockSpec((1, TM, w), lambda b, i: (b, i, 0))
    sds = jax.ShapeDtypeStruct
    per = TM // SUBLANES
    n8 = n_all // SUBLANES
    q, k, v, bg, sg, xl = pl.pallas_call(
        functools.partial(_rec_proj_kernel, nt=nt),
        grid=(bsz, nt),
        in_specs=[tok(d),
                  pl.BlockSpec((1, SUBLANES, d), lambda b, i: (b, jnp.maximum(i * per - 1, 0), 0)),
                  pl.BlockSpec((1, SUBLANES, d), lambda b, i: (b, jnp.minimum((i + 1) * per, n8 - 1), 0)),
                  _mod_spec(), full((d, REC_COLS)), full((SUBLANES, REC_CONV_COLS)), full((1, LRU_WIDTH)),
                  full((SUBLANES, LANES))],
        out_specs=[tok(GDN_WIDTH), tok(GDN_WIDTH), tok(GDN_WIDTH), tok(LANES), tok(2 * 512), tok(LRU_WIDTH)],
        out_shape=[sds((bsz, n_all, GDN_WIDTH), F32)] * 3 + [sds((bsz, n_all, LANES), F32),
                   sds((bsz, n_all, 2 * 512), BF16), sds((bsz, n_all, LRU_WIDTH), F32)],
        compiler_params=_params("parallel", "parallel"),
        name="rec_proj",
    )(xs, xs, xs, msel, win, conv_w, lru_conv_b[None, :], gparams)

    nch = n_all // GDN_CHUNK
    nctx = CTX_LEN // GDN_CHUNK
    fwd = lambda s: s
    bwd = lambda s: jnp.where(s < nctx, nctx - 1 - s, nch - 1 + nctx - s)
    prep = lambda order: (lambda b, s: (b, order(jnp.minimum(s, nch - 1)), 0))
    apply_ = lambda order: (lambda b, s: (b, order(jnp.maximum(s - 1, 0)), 0))
    chunk_specs = lambda cmap: [pl.BlockSpec((1, GDN_CHUNK, GDN_WIDTH), cmap)] * 3 + [pl.BlockSpec((1, GDN_CHUNK, LANES), cmap)]
    per_stream = (2, GDN_HEADS)
    outs = pl.pallas_call(
        _gdn_kernel,
        grid=(bsz, nch + 1),
        in_specs=chunk_specs(prep(fwd)) + chunk_specs(prep(bwd)),
        out_specs=[pl.BlockSpec((1, GDN_CHUNK, GDN_WIDTH), apply_(fwd)),
                   pl.BlockSpec((1, GDN_CHUNK, GDN_WIDTH), apply_(bwd))],
        out_shape=[sds((bsz, n_all, GDN_WIDTH), F32)] * 2,
        scratch_shapes=[pltpu.VMEM(per_stream + (GDN_HEAD_DIM, GDN_HEAD_DIM), F32),
                        pltpu.VMEM(per_stream + (GDN_CHUNK, GDN_HEAD_DIM), F32),
                        pltpu.VMEM(per_stream + (2, 2 * GDN_CHUNK, GDN_CHUNK), BF16),
                        pltpu.VMEM(per_stream + (SUBLANES, GDN_CHUNK), F32)],
        compiler_params=_params("parallel", "arbitrary"),
        name="gdn",
    )(q, k, v, bg, q, k, v, bg)

    half = LRU_WIDTH // 2
    r = pl.pallas_call(
        functools.partial(_lru_kernel, nt=nt),
        grid=(bsz, 2),
        in_specs=[pl.BlockSpec((1, n_all, half), lambda b, c: (b, 0, c)),
                  pl.BlockSpec((2, 2, 1, half, half), lambda b, c: (0, 0, c, 0, 0)),
                  pl.BlockSpec((2, 2, 1, 1, half), lambda b, c: (0, 0, c, 0, 0)),
                  pl.BlockSpec((2, half), lambda b, c: (0, c))],
        out_specs=pl.BlockSpec((1, n_all, half), lambda b, c: (b, 0, c)),
        out_shape=sds((bsz, n_all, LRU_WIDTH), F32),
        scratch_shapes=[pltpu.VMEM((TM, half), F32), pltpu.VMEM((TM, half), F32)],
        compiler_params=_params("parallel", "parallel"),
        name="rg_lru",
    )(xl, wg, lru_gate_b.reshape(2, 2, 2, 1, half), lru_lambda)

    off = 1 if last else 0
    tok_in = lambda w: pl.BlockSpec((1, TM, w), lambda b, i: (b, i + off, 0))
    return pl.pallas_call(
        _rec_out_kernel,
        grid=(bsz, nt - off),
        in_specs=[tok_in(GDN_WIDTH), tok_in(GDN_WIDTH), tok_in(LRU_WIDTH), tok_in(2 * 512), full((1, LANES)),
                  full((GDN_WIDTH + LRU_WIDTH, d)), tok_in(d), _mod_spec(off), full((1, d)), full((1, d))],
        out_specs=tok(d),
        out_shape=sds((bsz, n_all - off * TM, d), F32),
        compiler_params=_params("parallel", "parallel"),
        name="rec_out",
    )(outs[0], outs[1], r, sg, gdn_norm[None, :], w_out.astype(BF16), xs, msel, ln_g[None, :], ln_b[None, :])


def kernel(x, c, ctx, c_ctx, mod_w, mod_b, ln_g, ln_b, att_w_in, att_w_out, mla_q_norm, mla_w_uq, mla_kv_norm, mla_w_ukv, gqa_q_norm, gqa_k_norm, rec_w_in, rec_w_out, gdn_conv_w, gdn_a_log, gdn_dt_bias, gdn_norm, lru_conv_w, lru_conv_b, lru_gate_w, lru_gate_b, lru_lambda):
    bsz, n_seq, _ = x.shape
    assert ctx.shape[1] == CTX_LEN == TM and n_seq % TM == 0 and bsz < SUBLANES
    rope_a, rope_b = _rope_tables(n_seq)
    mods = _modulation(c, c_ctx, mod_w, mod_b)
    xs = (ctx, x)
    for layer in range(DEPTH):
        last = layer == DEPTH - 1
        li = layer // 2
        msel = _mod_select(mods[layer], bsz)
        if layer % 2 == 0:
            xs = _attention_layer(xs, msel, att_w_in[li], att_w_out[li], mla_q_norm[li], mla_w_uq[li],
                                  mla_kv_norm[li], mla_w_ukv[li], gqa_q_norm[li], gqa_k_norm[li],
                                  rope_a, rope_b, ln_g[layer], ln_b[layer], last)
        else:
            xs = _recurrent_layer(xs, msel, rec_w_in[li], rec_w_out[li], gdn_conv_w[li], gdn_a_log[li],
                                  gdn_dt_bias[li], gdn_norm[li], lru_conv_w[li], lru_conv_b[li],
                                  lru_gate_w[li], lru_gate_b[li], lru_lambda[li], ln_g[layer], ln_b[layer], last)
    return xs
```

```python
import functools

import numpy as np
import jax
import jax.numpy as jnp
from jax import lax
from jax.experimental import pallas as pl
from jax.experimental.pallas import tpu as pltpu

F32 = jnp.float32
BF16 = jnp.bfloat16

D_MODEL = 1024
DEPTH = 4
GRID_W = 64
CTX_LEN = 256
ROPE_THETA = 10000.0
EPS = 1e-6

MLA_HEADS = 8
MLA_Q_RANK = 256
MLA_KV_RANK = 128
MLA_NOPE = 64
MLA_ROPE = 32
MLA_V = 64
MLA_WIDTH = MLA_HEADS * MLA_V
GQA_HEADS = 8
GQA_KV_HEADS = 2
GQA_HEAD_DIM = 64
GQA_WIDTH = GQA_HEADS * GQA_HEAD_DIM
GDN_HEADS = 4
GDN_HEAD_DIM = 128
GDN_WIDTH = GDN_HEADS * GDN_HEAD_DIM
LRU_WIDTH = 512
LRU_BLOCKS = 8
LRU_BLOCK_W = LRU_WIDTH // LRU_BLOCKS
LRU_C = 8.0
DEEPNORM_ALPHA = (2 * DEPTH) ** 0.25

LANES = 128
SUBLANES = 8
TM = 256
GDN_CHUNK = 128
GDN_DIAG = 16
KEY_CHUNK = 512
PV_LAG = 4
LOG2_E = 1.4426950408889634
VMEM_LIMIT = 48 * 1024 * 1024

ATT_COLS = 2048
REC_CONV_COLS = 3 * GDN_WIDTH + LRU_WIDTH
REC_COLS = REC_CONV_COLS + 2 * 512 + LANES


def _params(*sem):
    return pltpu.CompilerParams(dimension_semantics=sem, vmem_limit_bytes=VMEM_LIMIT)


def _bdot(a, b):
    return jnp.dot(a.astype(BF16), b.astype(BF16), preferred_element_type=F32)


def _split_bf16(a, parts):
    out = []
    for _ in range(parts):
        hi = a.astype(BF16)
        out.append(hi)
        a = a - hi.astype(F32)
    return out


def _dot_split(a, b):
    a_hi, a_lo = _split_bf16(a, 2)
    b_hi, b_lo = _split_bf16(b, 2)
    d = functools.partial(jnp.dot, preferred_element_type=F32)
    return d(a_hi, b_hi) + (d(a_lo, b_hi) + d(a_hi, b_lo))


def _silu(x):
    return x * jax.nn.sigmoid(x)


def _lane_iota(shape):
    return lax.broadcasted_iota(jnp.int32, shape, len(shape) - 1)


def _swap_groups(x, n):
    fwd = pltpu.roll(x, LANES - n, 1)
    bwd = pltpu.roll(x, n, 1)
    return jnp.where((_lane_iota(x.shape) % (2 * n)) < n, fwd, bwd)


def _rope(x, cos, sin, n):
    return x * cos + _swap_groups(x, n) * sin


def _layer_norm(z, g, b):
    mu = jnp.mean(z, -1, keepdims=True)
    zc = z - mu
    var = jnp.mean(zc * zc, -1, keepdims=True)
    return zc * lax.rsqrt(var + EPS) * g + b


def _mod_kernel(c_ref, w_ref, b_ref, o_ref):
    c = c_ref[...]
    o_ref[0] = _dot_split(_silu(c), w_ref[0]) + b_ref[0]


def _modulation(c, c_ctx, mod_w, mod_b):
    bsz = c.shape[0]
    rows = jnp.concatenate([c, c_ctx[None, :], jnp.zeros((SUBLANES - bsz - 1, D_MODEL), F32)], 0)
    tn = 1024
    return pl.pallas_call(
        _mod_kernel,
        grid=(DEPTH, 3 * D_MODEL // tn),
        in_specs=[pl.BlockSpec((SUBLANES, D_MODEL), lambda l, n: (0, 0)),
                  pl.BlockSpec((1, D_MODEL, tn), lambda l, n: (l, 0, n)),
                  pl.BlockSpec((1, 1, tn), lambda l, n: (l, 0, n))],
        out_specs=pl.BlockSpec((1, SUBLANES, tn), lambda l, n: (l, 0, n)),
        out_shape=jax.ShapeDtypeStruct((DEPTH, SUBLANES, 3 * D_MODEL), F32),
        compiler_params=_params("parallel", "parallel"),
        name="modulation",
    )(rows, mod_w, mod_b.reshape(DEPTH, 1, 3 * D_MODEL))


def _mod_select(mods_l, bsz):
    m3 = mods_l.reshape(SUBLANES, 3, D_MODEL)
    ctx_m = jnp.broadcast_to(m3[bsz], (bsz, 3, D_MODEL))
    sel = jnp.stack([ctx_m, m3[:bsz]], 1)
    return jnp.pad(sel, ((0, 0), (0, 0), (0, SUBLANES - 3), (0, 0)))


def _mod_spec(off=0):
    return pl.BlockSpec((1, 1, SUBLANES, D_MODEL), lambda b, i: (b, jnp.minimum(i + off, 1), 0, 0))


def _token_tile(refs, n_streams):
    if n_streams == 1:
        return refs[0][0]
    return jnp.where(pl.program_id(1) == 0, refs[0][0], refs[1][0])


def _att_proj_kernel(*refs, n_streams):
    (m_ref, win_ref, wuq_ref, wk_ref, wv_ref, wqb_ref, nq_ref, nkv_ref, nqb_ref, nkb_ref, ra_ref, rat_ref, rb_ref,
     rbt_ref, qa_ref, ka_ref, va_ref, qb_ref, kb_ref, vb_ref, sg_ref) = refs[n_streams:]
    shift = m_ref[0, 0, 0:1, :]
    scale = m_ref[0, 0, 1:2, :]
    h = _token_tile(refs, n_streams) * (1.0 + scale) + shift
    p = _bdot(h, win_ref[...])
    cq, ckv, kr = p[:, 0:256], p[:, 256:384], p[:, 384:512]
    gates = p[:, 512:1536]
    kb, vb = p[:, 1536:1792], p[:, 1792:2048]
    sg_ref[0] = _silu(gates).astype(BF16)

    def rms(x):
        return x * lax.rsqrt(jnp.mean(x * x, -1, keepdims=True) + EPS)

    qa_t = _bdot(wuq_ref[...], (rms(cq) * nq_ref[...]).T)
    ckvn = rms(ckv) * nkv_ref[...]
    kn = _bdot(ckvn, wk_ref[...])
    va_t = _bdot(wv_ref[...], ckvn.T)
    for j in range(MLA_HEADS // 2):
        va_ref[0, j] = va_t[j * LANES:(j + 1) * LANES].astype(BF16)
    cos_a, sin_a = ra_ref[0], ra_ref[1]
    kr = _rope(kr, cos_a, sin_a, MLA_ROPE // 4)
    cos_t, sin_t = rat_ref[0], rat_ref[1]
    q_scale = LOG2_E * (MLA_NOPE + MLA_ROPE) ** -0.5
    r0, n8 = MLA_NOPE, MLA_ROPE // 4
    for hd in range(MLA_HEADS):
        sl = slice(hd * LANES, (hd + 1) * LANES)
        qt = qa_t[sl]
        swapped = jnp.concatenate([qt[:r0], qt[r0 + n8:r0 + 2 * n8], qt[r0:r0 + n8], qt[r0 + 3 * n8:r0 + 4 * n8],
                                   qt[r0 + 2 * n8:r0 + 3 * n8], qt[r0 + 4 * n8:]], 0)
        qa_ref[0, hd] = ((qt * cos_t + swapped * sin_t) * q_scale).astype(BF16)
        ka_ref[0, hd] = (kn[:, sl] + kr).astype(BF16)

    cos_b, sin_b = rb_ref[0], rb_ref[1]
    qb_t = _bdot(wqb_ref[...], h.T)
    cos_bt, sin_bt = rbt_ref[0, 0:GQA_HEAD_DIM, :], rbt_ref[1, 0:GQA_HEAD_DIM, :]
    n16 = GQA_HEAD_DIM // 4
    zeros = jnp.zeros((GQA_HEAD_DIM, TM), F32)
    for hd in range(GQA_HEADS):
        xq = qb_t[hd * GQA_HEAD_DIM:(hd + 1) * GQA_HEAD_DIM]
        xn = xq * lax.rsqrt(jnp.mean(xq * xq, 0, keepdims=True) + EPS) * nqb_ref[...]
        swapped = jnp.concatenate([xn[n16:2 * n16], xn[0:n16], xn[3 * n16:], xn[2 * n16:3 * n16]], 0)
        qt = (xn * cos_bt + swapped * sin_bt) * (LOG2_E * GQA_HEAD_DIM ** -0.5)
        qb_ref[0, hd] = jnp.concatenate([qt, zeros] if hd % 2 == 0 else [zeros, qt], 0).astype(BF16)
    for g in range(GQA_KV_HEADS):
        sl = slice(g * LANES, (g + 1) * LANES)
        kb_ref[0, g] = _rope(rms(kb[:, sl]) * nkb_ref[...], cos_b, sin_b, GQA_HEAD_DIM // 4).astype(BF16)
        vb_ref[0, g] = vb[:, sl].T.astype(BF16)


def _fold_rows(op, x, acc):
    for r in range(x.shape[0] // SUBLANES):
        acc = op(acc, x[r * SUBLANES:(r + 1) * SUBLANES])
    return acc


def _attend_heads(qtf, kf, vtf, s_refs, n_heads, nk):
    n_chunks = max(nk // KEY_CHUNK, 1)
    bounds = [i * KEY_CHUNK for i in range(n_chunks)] + [nk]
    chunks = [(bounds[i], bounds[i + 1] - bounds[i]) for i in range(n_chunks)]

    def score(h, c0, n, mt):
        s = jnp.dot(kf(h, c0, n), qtf(h), preferred_element_type=F32)
        s_refs[h % 2][c0:c0 + n, :] = s
        return _fold_rows(jnp.maximum, s, mt)

    neg = jnp.full((SUBLANES, TM), -jnp.inf, F32)
    zero = jnp.zeros((SUBLANES, TM), F32)
    acc = [None] * n_heads
    col_sum = [zero] * n_heads
    pending = []

    def pv_oldest():
        h, p, c0, n = pending.pop(0)
        o = jnp.dot(vtf(h, c0, n), p, preferred_element_type=F32)
        acc[h] = o if acc[h] is None else acc[h] + o

    mt = neg
    for c0, n in chunks:
        mt = score(0, c0, n, mt)
    for h in range(n_heads):
        m = jnp.max(mt, 0, keepdims=True)
        mt = neg
        for c0, n in chunks:
            p = jnp.exp2(s_refs[h % 2][c0:c0 + n, :] - m)
            col_sum[h] = _fold_rows(jnp.add, p, col_sum[h])
            pending.append((h, p.astype(BF16), c0, n))
            if h + 1 < n_heads:
                mt = score(h + 1, c0, n, mt)
            if len(pending) > PV_LAG:
                pv_oldest()
    while pending:
        pv_oldest()
    return [acc[h] / jnp.sum(col_sum[h], 0, keepdims=True) for h in range(n_heads)]


def _store_head_pairs(outs, sg_ref, o_ref):
    for j in range(len(outs) // 2):
        sl = slice(j * LANES, (j + 1) * LANES)
        pair = jnp.concatenate([outs[2 * j], outs[2 * j + 1]], 0).T
        o_ref[0, :, sl] = (pair * sg_ref[0, :, sl].astype(F32)).astype(BF16)


def _attn_kernel(qt_ref, k_ref, vt_ref, sg_ref, o_ref, sa_ref, sb_ref, *, n_all, n_heads, dv, k_of, v_of):
    def vt_chunk(h, c0, n):
        blk, r0 = v_of(h)
        return vt_ref[0, blk, r0:r0 + dv, c0:c0 + n]

    def run(nk):
        outs = _attend_heads(lambda h: qt_ref[0, h], lambda h, c0, n: k_ref[0, k_of(h), c0:c0 + n, :],
                             vt_chunk, (sa_ref, sb_ref), n_heads, nk)
        _store_head_pairs(outs, sg_ref, o_ref)

    @pl.when(pl.program_id(1) == 0)
    def _():
        run(CTX_LEN)

    @pl.when(pl.program_id(1) > 0)
    def _():
        run(n_all)


def _att_out_kernel(*refs, n_streams):
    ma_ref, mb_ref, w_ref, m_ref, g_ref, b_ref, o_ref = refs[n_streams:]
    y = (jnp.dot(ma_ref[0], w_ref[0:MLA_WIDTH, :], preferred_element_type=F32)
         + jnp.dot(mb_ref[0], w_ref[MLA_WIDTH:, :], preferred_element_type=F32))
    gate = m_ref[0, 0, 2:3, :]
    o_ref[0] = _layer_norm(DEEPNORM_ALPHA * _token_tile(refs, n_streams) + gate * y, g_ref[...], b_ref[...])


def _rope_tables(n_seq):
    pos = np.arange(n_seq)
    row = (pos // GRID_W).astype(np.float32)[:, None]
    col = (pos % GRID_W).astype(np.float32)[:, None]

    def quarter(dim):
        half = dim // 2
        inv = (ROPE_THETA ** (-np.arange(0, half, 2, dtype=np.float32) / half)).astype(np.float32)
        return np.cos(row * inv), np.sin(row * inv), np.cos(col * inv), np.sin(col * inv)

    def with_ctx(cos, sin):
        cos = np.concatenate([np.ones((CTX_LEN, LANES), np.float32), cos], 0)
        sin = np.concatenate([np.zeros((CTX_LEN, LANES), np.float32), sin], 0)
        return jnp.asarray(np.stack([cos, sin], 0).astype(np.float32))

    cr, sr, cc, sc = quarter(MLA_ROPE)
    ones = lambda n: np.ones((n_seq, n), np.float32)
    cos_a = np.concatenate([ones(MLA_NOPE), cr, cr, cc, cc, ones(32)], 1)
    sin_a = np.concatenate([0 * ones(MLA_NOPE), -sr, sr, -sc, sc, 0 * ones(32)], 1)
    cr, sr, cc, sc = quarter(GQA_HEAD_DIM)
    cos_b = np.concatenate([cr, cr, cc, cc] * 2, 1)
    sin_b = np.concatenate([-sr, sr, -sc, sc] * 2, 1)
    return with_ctx(cos_a, sin_a), with_ctx(cos_b, sin_b)


def _att_weights(w_in, w_uq, w_ukv):
    d = D_MODEL
    o = np.cumsum((0, MLA_Q_RANK, MLA_KV_RANK, MLA_ROPE, MLA_WIDTH, GQA_WIDTH, 128, 128, GQA_WIDTH))
    cq, ckv, kr, ga, qb, kb, vb, gb = (w_in[:, o[i]:o[i + 1]] for i in range(8))
    z = lambda n: jnp.zeros((d, n), F32)
    dup = lambda w: [w[:, 0:64], w[:, 0:64], w[:, 64:128], w[:, 64:128]]
    pieces = [cq, ckv, z(MLA_NOPE), kr, z(32), ga, gb] + dup(kb) + dup(vb)
    win = jnp.concatenate([p.astype(BF16) for p in pieces], 1)
    wq = w_uq.reshape(MLA_Q_RANK, MLA_HEADS, MLA_NOPE + MLA_ROPE)
    wq = jnp.pad(wq, ((0, 0), (0, 0), (0, 32))).reshape(MLA_Q_RANK, MLA_HEADS * LANES)
    wkv = w_ukv.reshape(MLA_KV_RANK, MLA_HEADS, MLA_NOPE + MLA_V)
    wk = jnp.pad(wkv[:, :, :MLA_NOPE], ((0, 0), (0, 0), (0, 64))).reshape(MLA_KV_RANK, MLA_HEADS * LANES)
    wv = wkv[:, :, MLA_NOPE:].reshape(MLA_KV_RANK, MLA_WIDTH)
    return win, wq.T.astype(BF16), wk.astype(BF16), wv.T.astype(BF16), qb.T.astype(BF16)


def _attention_layer(xs, msel, w_in, w_out, q_norm_a, w_uq, kv_norm_a, w_ukv, q_norm_b, k_norm_b,
                     rope_a, rope_b, ln_g, ln_b, last):
    streams = xs if isinstance(xs, tuple) else (xs,)
    bsz, d = streams[-1].shape[0], streams[-1].shape[2]
    n_all = sum(a.shape[1] for a in streams)
    nt = n_all // TM
    win, wq, wk, wv, wqb = _att_weights(w_in, w_uq, w_ukv)
    full = lambda shape: pl.BlockSpec(shape, lambda b, i: (0,) * len(shape))
    tok = lambda w: pl.BlockSpec((1, TM, w), lambda b, i: (b, i, 0))
    head = lambda n: pl.BlockSpec((1, n, TM, LANES), lambda b, i: (b, 0, i, 0))
    head_t = lambda n: pl.BlockSpec((1, n, LANES, TM), lambda b, i: (b, 0, 0, i))
    rope = pl.BlockSpec((2, TM, LANES), lambda b, i: (0, i, 0))
    rope_t = pl.BlockSpec((2, LANES, TM), lambda b, i: (0, 0, i))
    sds = jax.ShapeDtypeStruct

    def stream_specs(off):
        if len(streams) == 1:
            return [pl.BlockSpec((1, TM, d), lambda b, i: (b, i + off, 0))]
        return [pl.BlockSpec((1, TM, d), lambda b, i: (b, 0, 0)),
                pl.BlockSpec((1, TM, d), lambda b, i: (b, jnp.maximum(i - 1, 0), 0))]

    qa, ka, va, qb, kb, vb, sg = pl.pallas_call(
        functools.partial(_att_proj_kernel, n_streams=len(streams)),
        grid=(bsz, nt),
        in_specs=stream_specs(0) + [_mod_spec(), full((d, ATT_COLS)), full((MLA_HEADS * LANES, MLA_Q_RANK)),
                  full((MLA_KV_RANK, MLA_HEADS * LANES)), full((MLA_WIDTH, MLA_KV_RANK)), full((GQA_WIDTH, d)),
                  full((1, MLA_Q_RANK)), full((1, MLA_KV_RANK)), full((GQA_HEAD_DIM, 1)), full((1, LANES)),
                  rope, rope_t, rope, rope_t],
        out_specs=[head_t(MLA_HEADS), head(MLA_HEADS), head_t(MLA_HEADS // 2), head_t(GQA_HEADS),
                   head(GQA_KV_HEADS), head_t(GQA_KV_HEADS), tok(2 * 512)],
        out_shape=[sds((bsz, MLA_HEADS, LANES, n_all), BF16), sds((bsz, MLA_HEADS, n_all, LANES), BF16),
                   sds((bsz, MLA_HEADS // 2, LANES, n_all), BF16), sds((bsz, GQA_HEADS, LANES, n_all), BF16),
                   sds((bsz, GQA_KV_HEADS, n_all, LANES), BF16), sds((bsz, GQA_KV_HEADS, LANES, n_all), BF16),
                   sds((bsz, n_all, 2 * 512), BF16)],
        compiler_params=_params("parallel", "parallel"),
        name="att_proj",
    )(*streams, msel, win, wq, wk, wv, wqb, q_norm_a[None, :], kv_norm_a[None, :],
      q_norm_b[:, None], jnp.tile(k_norm_b, 2)[None, :],
      rope_a, jnp.swapaxes(rope_a, 1, 2), rope_b, jnp.swapaxes(rope_b, 1, 2))

    whole = lambda n: pl.BlockSpec((1, n, n_all, LANES), lambda b, i: (b, 0, 0, 0))
    whole_t = lambda n: pl.BlockSpec((1, n, LANES, n_all), lambda b, i: (b, 0, 0, 0))
    mix_a = pl.pallas_call(
        functools.partial(_attn_kernel, n_all=n_all, n_heads=MLA_HEADS, dv=MLA_V, k_of=lambda h: h,
                          v_of=lambda h: (h // 2, (h % 2) * MLA_V)),
        grid=(bsz, nt),
        in_specs=[head_t(MLA_HEADS), whole(MLA_HEADS), whole_t(MLA_HEADS // 2), tok(MLA_WIDTH)],
        out_specs=tok(MLA_WIDTH),
        out_shape=sds((bsz, n_all, MLA_WIDTH), BF16),
        scratch_shapes=[pltpu.VMEM((n_all, TM), F32)] * 2,
        compiler_params=_params("parallel", "parallel"),
        name="mla_attention",
    )(qa, ka, va, sg)
    group = GQA_HEADS // GQA_KV_HEADS
    mix_b = pl.pallas_call(
        functools.partial(_attn_kernel, n_all=n_all, n_heads=GQA_HEADS, dv=GQA_HEAD_DIM, k_of=lambda h: h // group,
                          v_of=lambda h: (h // group, 0)),
        grid=(bsz, nt),
        in_specs=[head_t(GQA_HEADS), whole(GQA_KV_HEADS), whole_t(GQA_KV_HEADS),
                  pl.BlockSpec((1, TM, GQA_WIDTH), lambda b, i: (b, i, 1))],
        out_specs=tok(GQA_WIDTH),
        out_shape=sds((bsz, n_all, GQA_WIDTH), BF16),
        scratch_shapes=[pltpu.VMEM((n_all, TM), F32)] * 2,
        compiler_params=_params("parallel", "parallel"),
        name="gqa_attention",
    )(qb, kb, vb, sg)

    off = 1 if last else 0
    tok_in = lambda w: pl.BlockSpec((1, TM, w), lambda b, i: (b, i + off, 0))
    assert not (last and len(streams) > 1)
    return pl.pallas_call(
        functools.partial(_att_out_kernel, n_streams=len(streams)),
        grid=(bsz, nt - off),
        in_specs=stream_specs(off) + [tok_in(MLA_WIDTH), tok_in(GQA_WIDTH), full((MLA_WIDTH + GQA_WIDTH, d)),
                                      _mod_spec(off), full((1, d)), full((1, d))],
        out_specs=tok(d),
        out_shape=sds((bsz, n_all - off * TM, d), F32),
        compiler_params=_params("parallel", "parallel"),
        name="att_out",
    )(*streams, mix_a, mix_b, w_out.astype(BF16), msel, ln_g[None, :], ln_b[None, :])


def _rec_proj_kernel(x_ref, xp_ref, xn_ref, m_ref, win_ref, cw_ref, cb_ref, gp_ref,
                     q_ref, k_ref, v_ref, bg_ref, sg_ref, xl_ref, *, nt):
    i = pl.program_id(1)
    shift = m_ref[0, 0, 0:1, :]
    scale = m_ref[0, 0, 1:2, :]
    mod = lambda x: x * (1.0 + scale) + shift
    prev_ok = (i >= 2).astype(F32)
    next_ok = jnp.logical_and(i >= 1, i < nt - 1).astype(F32)
    h = mod(x_ref[0])
    hext = jnp.concatenate([mod(xp_ref[0]) * prev_ok, h, mod(xn_ref[0]) * next_ok], 0)
    pext = _bdot(hext, win_ref[:, 0:REC_CONV_COLS])
    rest = _bdot(h, win_ref[:, REC_CONV_COLS:])
    n_ext = TM + 2 * SUBLANES
    y = cw_ref[1:2, :] * pext[SUBLANES:SUBLANES + TM]
    for j in (0, 2, 3):
        y = y + cw_ref[j:j + 1, :] * pltpu.roll(pext, (1 - j) % n_ext, 0)[SUBLANES:SUBLANES + TM]
    qkv = _silu(y[:, 0:3 * GDN_WIDTH])
    xl_ref[0] = y[:, 3 * GDN_WIDTH:] + cb_ref[...]
    for hd in range(GDN_HEADS):
        sl = slice(hd * LANES, (hd + 1) * LANES)
        q = qkv[:, sl]
        k = qkv[:, GDN_WIDTH + hd * LANES:GDN_WIDTH + (hd + 1) * LANES]
        q_ref[0, :, sl] = q * (lax.rsqrt(jnp.sum(q * q, -1, keepdims=True) + EPS) * GDN_HEAD_DIM ** -0.5)
        k_ref[0, :, sl] = k * lax.rsqrt(jnp.sum(k * k, -1, keepdims=True) + EPS)
    v_ref[0] = qkv[:, 2 * GDN_WIDTH:]
    sg_ref[0] = _silu(rest[:, 0:2 * 512]).astype(BF16)
    ba = rest[:, 2 * 512:]
    t = ba + gp_ref[1:2, :]
    softplus = jnp.maximum(t, 0.0) + jnp.log(1.0 + jnp.exp(-jnp.abs(t)))
    decay = -jnp.exp(gp_ref[0:1, :]) * softplus
    bg_ref[0] = jnp.where(_lane_iota(ba.shape) < 2 * GDN_HEADS, jax.nn.sigmoid(ba), decay)


def _each(f, *lists):
    return [f(*args) for args in zip(*lists)]


def _dots(xs, ys):
    return _each(_bdot, xs, ys)


def _stacked_dots(tops, bots, ys):
    n = tops[0].shape[0]
    both = [_bdot(jnp.concatenate([t, b], 0), y) for t, b, y in zip(tops, bots, ys)]
    return [o[:n] for o in both], [o[n:] for o in both]


def _tri_inverse(ms, strict_blocks, eye):
    mds = [jnp.where(strict_blocks, 0.0, m) for m in ms]
    ls = [jnp.where(strict_blocks, m, 0.0) for m in ms]
    add = lambda xs, ys: _each(lambda x, y: x + y, xs, ys)
    ps = [eye - md for md in mds]
    pws = _dots(mds, mds)
    for _ in range(2):
        pws, pn = _stacked_dots(pws, ps, pws)
        ps = add(ps, pn)
    dinvs = add(ps, _dots(ps, pws))
    es = _dots(dinvs, ls)
    qs = [eye - e for e in es]
    pws = _dots(es, es)
    pws, qn = _stacked_dots(pws, qs, pws)
    qs = add(qs, qn)
    qs = add(qs, _dots(qs, pws))
    return _dots(qs, dinvs)


def _gdn_kernel(*refs):
    c = GDN_CHUNK
    ins, outs = (refs[0:4], refs[4:8]), refs[8:10]
    s_ref, u_ref, ops_ref, dec_ref = refs[10:14]

    @pl.when(pl.program_id(1) == 0)
    def _():
        for ref in (s_ref, u_ref, ops_ref, dec_ref):
            ref[...] = jnp.zeros_like(ref)

    streams = [(d, hd) for d in range(2) for hd in range(GDN_HEADS)]
    lanes = lambda hd: slice(hd * LANES, (hd + 1) * LANES)
    ss = [s_ref[d, hd] for d, hd in streams]
    on_state = _dots([ops_ref[d, hd, 0] for d, hd in streams], ss)
    v_news = _each(lambda u, ws: u - ws[:c], [u_ref[d, hd] for d, hd in streams], on_state)
    on_new = _dots([ops_ref[d, hd, 1] for d, hd in streams], v_news)
    for (d, hd), s, a, b in zip(streams, ss, on_state, on_new):
        s_ref[d, hd] = s * dec_ref[d, hd, 0:1, :] + b[c:]
        outs[d][0, :, lanes(hd)] = a[c:] + b[:c]

    row_i = lax.broadcasted_iota(jnp.int32, (c, c), 0)
    col_i = lax.broadcasted_iota(jnp.int32, (c, c), 1)
    eye = (row_i == col_i).astype(F32)
    strict_blocks = (row_i // GDN_DIAG) != (col_i // GDN_DIAG)
    incls = [row_i >= col_i, row_i <= col_i]
    stricts = [row_i > col_i, row_i < col_i]
    lasts = [c - 1, 0]
    bgs = [ins[d][3][0] for d in range(2)]

    qs = [ins[d][0][0, :, lanes(hd)] for d, hd in streams]
    ks = [ins[d][1][0, :, lanes(hd)] for d, hd in streams]
    vs = [ins[d][2][0, :, lanes(hd)] for d, hd in streams]
    betas = [bgs[d][:, d * GDN_HEADS + hd:d * GDN_HEADS + hd + 1] for d, hd in streams]
    gs = [jnp.broadcast_to(bgs[d][:, (2 + d) * GDN_HEADS + hd:(2 + d) * GDN_HEADS + hd + 1], (c, c))
          for d, hd in streams]
    gcums = []
    for d in range(2):
        parts = [_split_bf16(g, 3) for g, (sd, _) in zip(gs, streams) if sd == d]
        wide = jnp.dot(incls[d].astype(BF16), jnp.concatenate([p for ps in parts for p in ps], 1),
                       preferred_element_type=F32)
        gcums += [wide[:, (3 * i) * c:(3 * i + 1) * c] + wide[:, (3 * i + 1) * c:(3 * i + 2) * c]
                  + wide[:, (3 * i + 2) * c:(3 * i + 3) * c] for i in range(GDN_HEADS)]
    gcum_ts = [g.T for g in gcums]
    g_lasts = [g[lasts[d]:lasts[d] + 1, :] for g, (d, _) in zip(gcums, streams)]
    decays = [jnp.exp(jnp.where(incls[d], g - gt, -1e30)) for g, gt, (d, _) in zip(gcums, gcum_ts, streams)]
    es = [jnp.exp(g) for g in gcums]
    kts = [k.T for k in ks]
    kbs = _each(lambda k, b: k * b, ks, betas)
    kks, qks = _stacked_dots(kbs, qs, kts)
    ms = [jnp.where(stricts[d], kk * dec, 0.0) for kk, dec, (d, _) in zip(kks, decays, streams)]
    attns = _each(lambda a, dec: a * dec, qks, decays)
    ts = _tri_inverse(ms, strict_blocks, eye)
    sols = _each(lambda t, v, b, kb, e: _bdot(t, jnp.concatenate([v * b, kb * e], 1)), ts, vs, betas, kbs, es)
    for (d, hd), sol, attn, q, e, kt, gl, gt in zip(streams, sols, attns, qs, es, kts, g_lasts, gcum_ts):
        u_ref[d, hd] = sol[:, :LANES]
        ops_ref[d, hd, 0, 0:c] = sol[:, LANES:].astype(BF16)
        ops_ref[d, hd, 0, c:2 * c] = (q * e).astype(BF16)
        ops_ref[d, hd, 1, 0:c] = attn.astype(BF16)
        ops_ref[d, hd, 1, c:2 * c] = (kt * jnp.exp(gl - gt)).astype(BF16)
        dec_ref[d, hd] = jnp.broadcast_to(jnp.exp(gl), (SUBLANES, c))


def _lru_kernel(xl_ref, wg_ref, bgate_ref, lam_ref, o_ref, af_ref, bf_ref, ab_ref, bb_ref, *, nt):
    w = xl_ref.shape[-1]
    rows = lax.broadcasted_iota(jnp.int32, (SUBLANES, w), 0)
    groups = TM // SUBLANES
    coeff_refs = ((af_ref, bf_ref), (ab_ref, bb_ref))
    sigmoid = lambda t: 0.5 * jnp.tanh(0.5 * t) + 0.5

    def coeff_body(t, carry):
        tile = pl.ds(pl.multiple_of(t * TM, TM), TM)
        x = xl_ref[0, tile, :]
        xb = x.astype(BF16)
        for d in range(2):
            lam = lam_ref[d:d + 1, :]
            log_sig = jnp.minimum(lam, 0.0) - jnp.log(1.0 + jnp.exp(-jnp.abs(lam)))
            r = sigmoid(jnp.dot(xb, wg_ref[d, 0, 0], preferred_element_type=F32) + bgate_ref[d, 0, 0])
            gi = sigmoid(jnp.dot(xb, wg_ref[d, 1, 0], preferred_element_type=F32) + bgate_ref[d, 1, 0])
            a = jnp.exp(LRU_C * r * log_sig)
            coeff_refs[d][0][tile, :] = a
            coeff_refs[d][1][tile, :] = jnp.sqrt(1.0 - a * a) * (gi * x)
        return carry

    lax.fori_loop(0, nt, coeff_body, 0)

    def scan8(a, bv, h_prev, rev):
        for sft in (1, 2, 4):
            if rev:
                ok = rows < SUBLANES - sft
                a_s = pltpu.roll(a, SUBLANES - sft, 0)
                b_s = pltpu.roll(bv, SUBLANES - sft, 0)
            else:
                ok = rows >= sft
                a_s = pltpu.roll(a, sft, 0)
                b_s = pltpu.roll(bv, sft, 0)
            bv = jnp.where(ok, a * b_s + bv, bv)
            a = jnp.where(ok, a * a_s, a)
        h = a * h_prev + bv
        last = h[0:1, :] if rev else h[SUBLANES - 1:SUBLANES, :]
        return h, jnp.broadcast_to(last, h.shape)

    def tile_body(s, carry):
        base_f = s * TM
        base_b = jnp.where(s == 0, 0, nt - s) * TM

        def group_body(g, c):
            hf, hb = c
            rf = pl.ds(pl.multiple_of(base_f + g * SUBLANES, SUBLANES), SUBLANES)
            rb = pl.ds(pl.multiple_of(base_b + (groups - 1 - g) * SUBLANES, SUBLANES), SUBLANES)
            h1, hf = scan8(af_ref[rf, :], bf_ref[rf, :], hf, False)
            h2, hb = scan8(ab_ref[rb, :], bb_ref[rb, :], hb, True)
            af_ref[rf, :] = h1
            ab_ref[rb, :] = h2
            return hf, hb

        return lax.fori_loop(0, groups, group_body, carry, unroll=2)

    zero = jnp.zeros((SUBLANES, w), F32)
    lax.fori_loop(0, nt, tile_body, (zero, zero))

    def sum_body(t, carry):
        tile = pl.ds(pl.multiple_of(t * TM, TM), TM)
        o_ref[0, tile, :] = af_ref[tile, :] + ab_ref[tile, :]
        return carry

    lax.fori_loop(0, nt, sum_body, 0)


def _rec_out_kernel(of_ref, ob_ref, r_ref, sg_ref, gn_ref, w_ref, x_ref, m_ref, g_ref, b_ref, o_ref):
    o = of_ref[0] + ob_ref[0]
    sg = sg_ref[0].astype(F32)
    y = jnp.dot((r_ref[0] * sg[:, GDN_WIDTH:]).astype(BF16), w_ref[GDN_WIDTH:, :], preferred_element_type=F32)
    for hd in range(GDN_HEADS):
        sl = slice(hd * LANES, (hd + 1) * LANES)
        oh = o[:, sl]
        og = oh * lax.rsqrt(jnp.mean(oh * oh, -1, keepdims=True) + EPS) * gn_ref[...] * sg[:, sl]
        y = y + jnp.dot(og.astype(BF16), w_ref[sl, :], preferred_element_type=F32)
    gate = m_ref[0, 0, 2:3, :]
    o_ref[0] = _layer_norm(DEEPNORM_ALPHA * x_ref[0] + gate * y, g_ref[...], b_ref[...])


def _rec_weights(w_in, gdn_conv_w, lru_conv_w, gdn_a_log, gdn_dt_bias, lru_gate_w):
    d = D_MODEL
    o = np.cumsum((0, 3 * GDN_WIDTH, GDN_WIDTH, 2 * GDN_HEADS, 2 * GDN_HEADS, LRU_WIDTH, LRU_WIDTH))
    qkv, z, b, a, xr, gr = (w_in[:, o[i]:o[i + 1]] for i in range(6))
    pieces = [qkv, xr, z, gr, b, a, jnp.zeros((d, LANES - 4 * GDN_HEADS), F32)]
    win = jnp.concatenate([p.astype(BF16) for p in pieces], 1)
    conv_w = jnp.pad(jnp.concatenate([gdn_conv_w, lru_conv_w], 1), ((0, SUBLANES - 4), (0, 0)))
    pad = lambda p: jnp.pad(p.reshape(-1), (2 * GDN_HEADS, LANES - 4 * GDN_HEADS))
    gparams = jnp.pad(jnp.stack([pad(gdn_a_log), pad(gdn_dt_bias)], 0), ((0, SUBLANES - 2), (0, 0)))
    half = LRU_WIDTH // 2
    blocks = lru_gate_w.reshape(2, 2, 2, LRU_BLOCKS // 2, LRU_BLOCK_W, LRU_BLOCK_W)
    eye = jnp.eye(LRU_BLOCKS // 2, dtype=F32)
    wg = jnp.einsum('dghncm,nk->dghnckm', blocks, eye).reshape(2, 2, 2, half, half)
    return win, conv_w, gparams, wg.astype(BF16)


def _recurrent_layer(xs, msel, w_in, w_out, gdn_conv_w, gdn_a_log, gdn_dt_bias, gdn_norm, lru_conv_w,
                     lru_conv_b, lru_gate_w, lru_gate_b, lru_lambda, ln_g, ln_b, last):
    bsz, n_all, d = xs.shape
    nt = n_all // TM
    win, conv_w, gparams, wg = _rec_weights(w_in, gdn_conv_w, lru_conv_w, gdn_a_log, gdn_dt_bias, lru_gate_w)
    full = lambda shape: pl.BlockSpec(shape, lambda b, i: (0,) * len(shape))
    tok = lambda w: pl.BlockSpec((1, TM, w), lambda b, i: (b, i, 0))
    sds = jax.ShapeDtypeStruct
    per = TM // SUBLANES
    n8 = n_all // SUBLANES
    q, k, v, bg, sg, xl = pl.pallas_call(
        functools.partial(_rec_proj_kernel, nt=nt),
        grid=(bsz, nt),
        in_specs=[tok(d),
                  pl.BlockSpec((1, SUBLANES, d), lambda b, i: (b, jnp.maximum(i * per - 1, 0), 0)),
                  pl.BlockSpec((1, SUBLANES, d), lambda b, i: (b, jnp.minimum((i + 1) * per, n8 - 1), 0)),
                  _mod_spec(), full((d, REC_COLS)), full((SUBLANES, REC_CONV_COLS)), full((1, LRU_WIDTH)),
                  full((SUBLANES, LANES))],
        out_specs=[tok(GDN_WIDTH), tok(GDN_WIDTH), tok(GDN_WIDTH), tok(LANES), tok(2 * 512), tok(LRU_WIDTH)],
        out_shape=[sds((bsz, n_all, GDN_WIDTH), F32)] * 3 + [sds((bsz, n_all, LANES), F32),
                   sds((bsz, n_all, 2 * 512), BF16), sds((bsz, n_all, LRU_WIDTH), F32)],
        compiler_params=_params("parallel", "parallel"),
        name="rec_proj",
    )(xs, xs, xs, msel, win, conv_w, lru_conv_b[None, :], gparams)

    nch = n_all // GDN_CHUNK
    nctx = CTX_LEN // GDN_CHUNK
    fwd = lambda s: s
    bwd = lambda s: jnp.where(s < nctx, nctx - 1 - s, nch - 1 + nctx - s)
    prep = lambda order: (lambda b, s: (b, order(jnp.minimum(s, nch - 1)), 0))
    apply_ = lambda order: (lambda b, s: (b, order(jnp.maximum(s - 1, 0)), 0))
    chunk_specs = lambda cmap: [pl.BlockSpec((1, GDN_CHUNK, GDN_WIDTH), cmap)] * 3 + [pl.BlockSpec((1, GDN_CHUNK, LANES), cmap)]
    per_stream = (2, GDN_HEADS)
    outs = pl.pallas_call(
        _gdn_kernel,
        grid=(bsz, nch + 1),
        in_specs=chunk_specs(prep(fwd)) + chunk_specs(prep(bwd)),
        out_specs=[pl.BlockSpec((1, GDN_CHUNK, GDN_WIDTH), apply_(fwd)),
                   pl.BlockSpec((1, GDN_CHUNK, GDN_WIDTH), apply_(bwd))],
        out_shape=[sds((bsz, n_all, GDN_WIDTH), F32)] * 2,
        scratch_shapes=[pltpu.VMEM(per_stream + (GDN_HEAD_DIM, GDN_HEAD_DIM), F32),
                        pltpu.VMEM(per_stream + (GDN_CHUNK, GDN_HEAD_DIM), F32),
                        pltpu.VMEM(per_stream + (2, 2 * GDN_CHUNK, GDN_CHUNK), BF16),
                        pltpu.VMEM(per_stream + (SUBLANES, GDN_CHUNK), F32)],
        compiler_params=_params("parallel", "arbitrary"),
        name="gdn",
    )(q, k, v, bg, q, k, v, bg)

    half = LRU_WIDTH // 2
    r = pl.pallas_call(
        functools.partial(_lru_kernel, nt=nt),
        grid=(bsz, 2),
        in_specs=[pl.BlockSpec((1, n_all, half), lambda b, c: (b, 0, c)),
                  pl.BlockSpec((2, 2, 1, half, half), lambda b, c: (0, 0, c, 0, 0)),
                  pl.BlockSpec((2, 2, 1, 1, half), lambda b, c: (0, 0, c, 0, 0)),
                  pl.BlockSpec((2, half), lambda b, c: (0, c))],
        out_specs=pl.BlockSpec((1, n_all, half), lambda b, c: (b, 0, c)),
        out_shape=sds((bsz, n_all, LRU_WIDTH), F32),
        scratch_shapes=[pltpu.VMEM((n_all, half), F32)] * 4,
        compiler_params=_params("parallel", "parallel"),
        name="rg_lru",
    )(xl, wg, lru_gate_b.reshape(2, 2, 2, 1, half), lru_lambda)

    off = 1 if last else 0
    tok_in = lambda w: pl.BlockSpec((1, TM, w), lambda b, i: (b, i + off, 0))
    return pl.pallas_call(
        _rec_out_kernel,
        grid=(bsz, nt - off),
        in_specs=[tok_in(GDN_WIDTH), tok_in(GDN_WIDTH), tok_in(LRU_WIDTH), tok_in(2 * 512), full((1, LANES)),
                  full((GDN_WIDTH + LRU_WIDTH, d)), tok_in(d), _mod_spec(off), full((1, d)), full((1, d))],
        out_specs=tok(d),
        out_shape=sds((bsz, n_all - off * TM, d), F32),
        compiler_params=_params("parallel", "parallel"),
        name="rec_out",
    )(outs[0], outs[1], r, sg, gdn_norm[None, :], w_out.astype(BF16), xs, msel, ln_g[None, :], ln_b[None, :])


def kernel(x, c, ctx, c_ctx, mod_w, mod_b, ln_g, ln_b, att_w_in, att_w_out, mla_q_norm, mla_w_uq, mla_kv_norm, mla_w_ukv, gqa_q_norm, gqa_k_norm, rec_w_in, rec_w_out, gdn_conv_w, gdn_a_log, gdn_dt_bias, gdn_norm, lru_conv_w, lru_conv_b, lru_gate_w, lru_gate_b, lru_lambda):
    bsz, n_seq, _ = x.shape
    assert ctx.shape[1] == CTX_LEN == TM and n_seq % TM == 0 and bsz < SUBLANES
    rope_a, rope_b = _rope_tables(n_seq)
    mods = _modulation(c, c_ctx, mod_w, mod_b)
    xs = (ctx, x)
    for layer in range(DEPTH):
        last = layer == DEPTH - 1
        li = layer // 2
        msel = _mod_select(mods[layer], bsz)
        if layer % 2 == 0:
            xs = _attention_layer(xs, msel, att_w_in[li], att_w_out[li], mla_q_norm[li], mla_w_uq[li],
                                  mla_kv_norm[li], mla_w_ukv[li], gqa_q_norm[li], gqa_k_norm[li],
                                  rope_a, rope_b, ln_g[layer], ln_b[layer], last)
        else:
            xs = _recurrent_layer(xs, msel, rec_w_in[li], rec_w_out[li], gdn_conv_w[li], gdn_a_log[li],
                                  gdn_dt_bias[li], gdn_norm[li], lru_conv_w[li], lru_conv_b[li],
                                  lru_gate_w[li], lru_gate_b[li], lru_lambda[li], ln_g[layer], ln_b[layer], last)
    return xs
```
